```python
import jax, jax.numpy as jnp
from jax import lax
import numpy as np

D_MODEL = 1024
BATCH = 8
SEQ = 8192
DEPTH = 1

GRID_W = 64
N_Q_HEADS = 16
N_KV_HEADS = 4
HEAD_DIM = 64
ROPE_THETA = 10000.0
Q_BLOCK = 128
SSD_EXPAND = 2
D_INNER = SSD_EXPAND * D_MODEL
SSD_HEAD_DIM = 64
N_SSD_HEADS = D_INNER // SSD_HEAD_DIM
N_SSD_GROUPS = 4
D_STATE = 128
D_CONV = 5
CHUNK = 128
D_FF = 4 * D_MODEL
EPS = 1e-6

ATTN_Q_DIM = N_Q_HEADS * HEAD_DIM
ATTN_KV_DIM = N_KV_HEADS * HEAD_DIM
CONV_DIM = D_INNER + 2 * N_SSD_GROUPS * D_STATE
D_IN_PROJ = ATTN_Q_DIM + 2 * ATTN_KV_DIM + CONV_DIM + D_INNER + 2 * N_SSD_HEADS + 2 * D_MODEL

kernel_name = "hybrid_gqa_ssd_griffin_merge_block"


def rms_norm(x, w):
    xf = x.astype(jnp.float32)
    xf = xf * lax.rsqrt(jnp.mean(xf * xf, axis=-1, keepdims=True) + EPS)
    return xf.astype(x.dtype) * w


def rotate(u, cos, sin):
    f = u.shape[-1] // 2
    u1, u2 = u[..., :f], u[..., f:]
    cos = cos[None, :, None, :]
    sin = sin[None, :, None, :]
    return jnp.concatenate([u1 * cos - u2 * sin, u2 * cos + u1 * sin], axis=-1)


def axial_rope(t, cos_r, sin_r, cos_c, sin_c):
    half = t.shape[-1] // 2
    out = jnp.concatenate([rotate(t[..., :half], cos_r, sin_r),
                           rotate(t[..., half:], cos_c, sin_c)], axis=-1)
    return out.astype(t.dtype)


def block_attention(q, k, v):
    b, s, hq, dh = q.shape
    hkv = k.shape[2]
    rep = hq // hkv
    nb = s // Q_BLOCK
    qb = q.reshape(b, nb, Q_BLOCK, hkv, rep, dh).transpose(1, 0, 2, 3, 4, 5)
    scale = dh ** -0.5

    def one_block(qi):
        sc = jnp.einsum("bqgrd,bkgd->bgrqk", qi, k).astype(jnp.float32) * scale
        p = jax.nn.softmax(sc, axis=-1).astype(v.dtype)
        return jnp.einsum("bgrqk,bkgd->bqgrd", p, v)

    out = lax.map(one_block, qb)
    return out.transpose(1, 0, 2, 3, 4, 5).reshape(b, s, hq * dh)


def segsum(a):
    t = a.shape[-1]
    cs = jnp.cumsum(a, axis=-1)
    diff = cs[..., :, None] - cs[..., None, :]
    mask = jnp.tril(jnp.ones((t, t), dtype=bool))
    return jnp.where(mask, diff, -jnp.inf)


def ssd_chunked(xdt, a, bm, cm):
    b, s, h, p = xdt.shape
    g, n = bm.shape[2], bm.shape[3]
    r = h // g
    nc = s // CHUNK
    X = xdt.astype(jnp.float32).reshape(b, nc, CHUNK, g, r, p)
    A = a.reshape(b, nc, CHUNK, g, r).transpose(0, 3, 4, 1, 2)
    Bc = bm.astype(jnp.float32).reshape(b, nc, CHUNK, g, n)
    Cc = cm.astype(jnp.float32).reshape(b, nc, CHUNK, g, n)
    A_cs = jnp.cumsum(A, axis=-1)
    CB = jnp.einsum("bclgn,bcsgn->bgcls", Cc, Bc)
    M = CB[:, :, None] * jnp.exp(segsum(A))
    y_diag = jnp.einsum("bgrcls,bcsgrp->bclgrp", M, X)
    decay_states = jnp.exp(A_cs[..., -1:] - A_cs)
    states = jnp.einsum("bclgn,bgrcl,bclgrp->cbgrpn", Bc, decay_states, X)
    chunk_decay = jnp.moveaxis(jnp.exp(A_cs[..., -1]), -1, 0)

    def step(hs, inp):
        st, dec = inp
        return dec[..., None, None] * hs + st, hs

    h0 = jnp.zeros(states.shape[1:], jnp.float32)
    _, prev = lax.scan(step, h0, (states, chunk_decay))
    y_off = jnp.einsum("bclgn,cbgrpn,bgrcl->bclgrp", Cc, prev, jnp.exp(A_cs))
    return (y_diag + y_off).reshape(b, s, h, p)


def depthwise_conv_centred(u, w, bias):
    pad = (w.shape[0] - 1) // 2
    out = lax.conv_general_dilated(u, w[:, None, :].astype(u.dtype), window_strides=(1,),
                                   padding=[(pad, pad)], dimension_numbers=("NWC", "WIO", "NWC"),
                                   feature_group_count=u.shape[-1])
    return out + bias


def ssd_mixer(xBC, z, dt_raw, conv_w, conv_b, A_log, dt_bias, ssd_D, ssd_norm_w):
    b, s, _ = xBC.shape
    xBC = jax.nn.silu(depthwise_conv_centred(xBC, conv_w, conv_b))
    gn = N_SSD_GROUPS * D_STATE
    xs = xBC[..., :D_INNER].reshape(b, s, N_SSD_HEADS, SSD_HEAD_DIM)
    bm = xBC[..., D_INNER:D_INNER + gn].reshape(b, s, N_SSD_GROUPS, D_STATE)
    cm = xBC[..., D_INNER + gn:].reshape(b, s, N_SSD_GROUPS, D_STATE)
    dt = jax.nn.softplus(dt_raw.astype(jnp.float32).reshape(b, s, 2, N_SSD_HEADS)
                         + dt_bias.astype(jnp.float32))
    A = -jnp.exp(A_log.astype(jnp.float32))
    dt_f, dt_b = dt[:, :, 0], dt[:, :, 1]
    xf = xs.astype(jnp.float32)
    y_fwd = ssd_chunked(xf * dt_f[..., None], dt_f * A[0], bm, cm)
    flip = lambda t: jnp.flip(t, axis=1)
    y_bwd = flip(ssd_chunked(flip(xf * dt_b[..., None]), flip(dt_b * A[1]), flip(bm), flip(cm)))
    y = y_fwd + y_bwd + ssd_D.astype(jnp.float32)[:, None] * xf
    y = y.reshape(b, s, D_INNER).astype(xBC.dtype)
    return rms_norm(y * jax.nn.silu(z), ssd_norm_w)


def setup_inputs(seed: int = 0) -> dict:
    key = jax.random.key(seed)
    ks = jax.random.split(key, 20)
    f32 = jnp.float32
    L = DEPTH
    nrm = lambda k, shape, s: jax.random.normal(k, shape, f32) * s
    x = jax.random.normal(ks[0], (BATCH, SEQ, D_MODEL), f32)
    c = jax.random.normal(ks[1], (BATCH, D_MODEL), f32)
    w_ada = nrm(ks[2], (L, D_MODEL, 6 * D_MODEL), 0.5 * D_MODEL ** -0.5)
    b_ada = nrm(ks[3], (L, 6 * D_MODEL), 0.02)
    norm1_w = 1.0 + nrm(ks[4], (L, D_MODEL), 0.02)
    norm2_w = 1.0 + nrm(ks[5], (L, D_MODEL), 0.02)
    w_in = nrm(ks[6], (L, D_MODEL, D_IN_PROJ), D_MODEL ** -0.5)
    q_norm_w = 1.0 + nrm(ks[7], (L, HEAD_DIM), 0.02)
    k_norm_w = 1.0 + nrm(ks[8], (L, HEAD_DIM), 0.02)
    conv_w = nrm(ks[9], (L, D_CONV, CONV_DIM), D_CONV ** -0.5)
    conv_b = nrm(ks[10], (L, CONV_DIM), 0.02)
    A_log = jnp.log(jax.random.uniform(ks[11], (L, 2, N_SSD_HEADS), f32, 1.0, 16.0))
    dt0 = jnp.exp(jax.random.uniform(ks[12], (L, 2, N_SSD_HEADS), f32, np.log(1e-3), np.log(1e-1)))
    dt_bias = dt0 + jnp.log(-jnp.expm1(-dt0))
    ssd_D = 1.0 + nrm(ks[13], (L, N_SSD_HEADS), 0.02)
    ssd_norm_w = 1.0 + nrm(ks[14], (L, D_INNER), 0.02)
    w_attn_out = nrm(ks[15], (L, ATTN_Q_DIM, D_MODEL), ATTN_Q_DIM ** -0.5)
    w_ssd_out = nrm(ks[16], (L, D_INNER, D_MODEL), D_INNER ** -0.5)
    w_o = nrm(ks[17], (L, D_MODEL, D_MODEL), D_MODEL ** -0.5)
    w_mlp1 = nrm(ks[18], (L, D_MODEL, D_FF), D_MODEL ** -0.5)
    w_mlp2 = nrm(ks[19], (L, D_FF, D_MODEL), D_FF ** -0.5)
    return {"x": x, "c": c, "w_ada": w_ada, "b_ada": b_ada, "norm1_w": norm1_w, "norm2_w": norm2_w,
            "w_in": w_in, "q_norm_w": q_norm_w, "k_norm_w": k_norm_w, "conv_w": conv_w, "conv_b": conv_b,
            "A_log": A_log, "dt_bias": dt_bias, "ssd_D": ssd_D, "ssd_norm_w": ssd_norm_w,
            "w_attn_out": w_attn_out, "w_ssd_out": w_ssd_out, "w_o": w_o,
            "w_mlp1": w_mlp1, "w_mlp2": w_mlp2}


def reference(x, c, w_ada, b_ada, norm1_w, norm2_w, w_in, q_norm_w, k_norm_w, conv_w, conv_b,
              A_log, dt_bias, ssd_D, ssd_norm_w, w_attn_out, w_ssd_out, w_o, w_mlp1, w_mlp2):
    b, s, d = x.shape
    rows = s // GRID_W
    pos_row = jnp.repeat(jnp.arange(rows, dtype=jnp.int32), GRID_W).astype(jnp.float32)
    pos_col = jnp.tile(jnp.arange(GRID_W, dtype=jnp.int32), rows).astype(jnp.float32)
    axis_dim = HEAD_DIM // 2
    inv_freq = ROPE_THETA ** (-jnp.arange(0, axis_dim, 2, dtype=jnp.float32) / axis_dim)
    ang_r = pos_row[:, None] * inv_freq[None, :]
    ang_c = pos_col[:, None] * inv_freq[None, :]
    cos_r, sin_r = jnp.cos(ang_r), jnp.sin(ang_r)
    cos_c, sin_c = jnp.cos(ang_c), jnp.sin(ang_c)

    sizes = [ATTN_Q_DIM, ATTN_KV_DIM, ATTN_KV_DIM, CONV_DIM, D_INNER, 2 * N_SSD_HEADS, 2 * D_MODEL]
    offsets = []
    acc = 0
    for sz in sizes[:-1]:
        acc += sz
        offsets.append(acc)

    for l in range(DEPTH):
        mod = jax.nn.silu(c) @ w_ada[l] + b_ada[l]
        shift1, scale1, gate1, shift2, scale2, gate2 = [m[:, None, :] for m in jnp.split(mod, 6, axis=-1)]

        h = rms_norm(x, norm1_w[l]) * (1.0 + scale1) + shift1
        proj = h @ w_in[l]
        q, k, v, xBC, z, dt_raw, gates = jnp.split(proj, offsets, axis=-1)

        q = rms_norm(q.reshape(b, s, N_Q_HEADS, HEAD_DIM), q_norm_w[l])
        k = rms_norm(k.reshape(b, s, N_KV_HEADS, HEAD_DIM), k_norm_w[l])
        v = v.reshape(b, s, N_KV_HEADS, HEAD_DIM)
        q = axial_rope(q, cos_r, sin_r, cos_c, sin_c)
        k = axial_rope(k, cos_r, sin_r, cos_c, sin_c)
        attn = block_attention(q, k, v)

        ssd = ssd_mixer(xBC, z, dt_raw, conv_w[l], conv_b[l], A_log[l], dt_bias[l], ssd_D[l], ssd_norm_w[l])

        g_attn = jax.nn.sigmoid(gates[..., :D_MODEL])
        g_ssd = jax.nn.sigmoid(gates[..., D_MODEL:])
        merged = g_attn * (attn @ w_attn_out[l]) + g_ssd * (ssd @ w_ssd_out[l])
        x = x + gate1 * (merged @ w_o[l])

        h2 = rms_norm(x, norm2_w[l]) * (1.0 + scale2) + shift2
        ff = jnp.square(jax.nn.relu(h2 @ w_mlp1[l])) @ w_mlp2[l]
        x = x + gate2 * ff
    return x
```

```python
import functools
import math

import jax
import jax.numpy as jnp
import numpy as np
from jax import lax
from jax.experimental import pallas as pl
from jax.experimental.pallas import tpu as pltpu

F32 = jnp.float32
BF16 = jnp.bfloat16

D_MODEL = 1024
GRID_W = 64
N_Q_HEADS = 16
N_KV_HEADS = 4
REP = N_Q_HEADS // N_KV_HEADS
HEAD_DIM = 64
ROPE_THETA = 10000.0
D_INNER = 2048
SSD_P = 64
SSD_H = D_INNER // SSD_P
SSD_G = 4
SSD_N = 128
HEADS_PER_GROUP = SSD_H // SSD_G
D_CONV = 5
CHUNK = 128
CONV_DIM = D_INNER + 2 * SSD_G * SSD_N
D_FF = 4 * D_MODEL
EPS = 1e-6

LANES = 128
V_EXT = 128
CONV_HALO = 16
NEG_BIG = -1e30
VMEM_LIMIT = 56 * 1024 * 1024


def _dot(a, b):
    return jnp.dot(a, b, preferred_element_type=F32)


def _dot_nt(a, b):
    return lax.dot_general(a, b, (((1,), (1,)), ((), ())), preferred_element_type=F32)


def _split3(v):
    hi = v.astype(BF16)
    r = v - hi.astype(F32)
    mid = r.astype(BF16)
    lo = (r - mid.astype(F32)).astype(BF16)
    return hi, mid, lo


def _const_spec(shape):
    return pl.BlockSpec(shape, lambda *_: (0,) * len(shape))


def _ada_kernel(c_ref, w_ref, b_ref, o_ref):
    c = c_ref[...]
    sc = c * jax.nn.sigmoid(c)
    a_hi, a_mid, a_lo = _split3(sc)
    w_hi, w_mid, w_lo = _split3(w_ref[...])
    acc = _dot(a_hi, w_hi)
    acc += _dot(a_hi, w_mid) + _dot(a_mid, w_hi)
    acc += _dot(a_hi, w_lo) + _dot(a_mid, w_mid) + _dot(a_lo, w_hi)
    o_ref[...] = acc + b_ref[...]


def _ada(c, w_ada, b_ada):
    b, d = c.shape
    n = w_ada.shape[1]
    tn = 1536
    return pl.pallas_call(
        _ada_kernel,
        grid=(n // tn,),
        in_specs=[
            pl.BlockSpec((b, d), lambda j: (0, 0)),
            pl.BlockSpec((d, tn), lambda j: (0, j)),
            pl.BlockSpec((1, tn), lambda j: (0, j)),
        ],
        out_specs=pl.BlockSpec((b, tn), lambda j: (0, j)),
        out_shape=jax.ShapeDtypeStruct((b, n), F32),
        compiler_params=pltpu.CompilerParams(dimension_semantics=("arbitrary",), vmem_limit_bytes=VMEM_LIMIT),
        name="ada",
    )(c, w_ada, b_ada.reshape(1, n))


def _norm_rope_block(t, ss, ta, tb, swap_lo):
    n = t * lax.rsqrt(ss + EPS)
    fwd = pltpu.roll(n, LANES - 16, axis=1)
    bwd = pltpu.roll(n, 16, axis=1)
    sw = jnp.where(swap_lo, fwd, bwd)
    return n * ta + sw * tb


def _inproj_kernel(x_ref, shift_ref, scale_ref, nw_ref, wq_ref, wk_ref, wv_ref, wx_ref, wz_ref, wdt_ref, wg_ref,
                   gsum_ref, qa_ref, qb_ref, ka_ref, kb_ref, vone_ref, dtb_ref,
                   q_out, kt_out, v_out, xbc_out, z_out, dt_out, g_out):
    x = x_ref[...]
    ts = x.shape[0]
    ms = jnp.mean(x * x, axis=-1, keepdims=True)
    xn = x * lax.rsqrt(ms + EPS)
    h = (xn * (nw_ref[...] * (1.0 + scale_ref[...])) + shift_ref[...]).astype(BF16)

    lane = lax.broadcasted_iota(jnp.int32, (ts, LANES), 1)
    swap_lo = (lane & 31) < 16
    gsum = gsum_ref[...]

    qf = _dot(h, wq_ref[...])
    qa = qa_ref[...]
    qb = qb_ref[...]
    for blk in range(N_Q_HEADS * HEAD_DIM // 256):
        t = qf[:, blk * 256:(blk + 1) * 256]
        ss = _dot((t * t).astype(BF16), gsum)
        for half in range(2):
            sl = slice(half * LANES, (half + 1) * LANES)
            r = _norm_rope_block(t[:, sl], ss[:, sl], qa, qb, swap_lo).astype(BF16)
            hd = blk * 4 + half * 2
            q_out[hd] = r[:, :HEAD_DIM]
            q_out[hd + 1] = r[:, HEAD_DIM:]

    kf = _dot(h, wk_ref[...])
    ssk = _dot((kf * kf).astype(BF16), gsum)
    ka = ka_ref[...]
    kb = kb_ref[...]
    for half in range(2):
        sl = slice(half * LANES, (half + 1) * LANES)
        r = _norm_rope_block(kf[:, sl], ssk[:, sl], ka, kb, swap_lo)
        kt_out[sl, :] = r.T.astype(BF16)

    v_out[...] = (_dot(h, wv_ref[...]) + vone_ref[...]).astype(BF16)

    for blk in range(CONV_DIM // 512):
        sl = slice(blk * 512, (blk + 1) * 512)
        xbc_out[:, sl] = _dot(h, wx_ref[:, sl]).astype(BF16)

    for blk in range(D_INNER // 512):
        sl = slice(blk * 512, (blk + 1) * 512)
        zf = _dot(h, wz_ref[:, sl])
        z_out[:, sl] = (zf * jax.nn.sigmoid(zf)).astype(BF16)

    dtf = _dot(h, wdt_ref[...]) + dtb_ref[...]
    dt_out[...] = jnp.maximum(dtf, 0.0) + jnp.log1p(jnp.exp(-jnp.abs(dtf)))

    for blk in range(2 * D_MODEL // 512):
        sl = slice(blk * 512, (blk + 1) * 512)
        g_out[:, sl] = jax.nn.sigmoid(_dot(h, wg_ref[:, sl])).astype(BF16)


def _inproj(x, shift1, scale1, norm1_w, wq, wk, wv, wx, wz, wdt, wg, gsum, qa, qb, ka, kb, vone, dtb, ts):
    b, s, d = x.shape
    n_t = s // ts
    row = lambda bi, i: (bi, i, 0)
    per_b = pl.BlockSpec((None, 1, d), lambda bi, i: (bi, 0, 0))
    tab = pl.BlockSpec((ts, LANES), lambda bi, i: (i, 0))
    in_specs = [
        pl.BlockSpec((None, ts, d), row), per_b, per_b, _const_spec((1, d)),
        _const_spec(wq.shape), _const_spec(wk.shape), _const_spec(wv.shape), _const_spec(wx.shape),
        _const_spec(wz.shape), _const_spec(wdt.shape), _const_spec(wg.shape), _const_spec(gsum.shape),
        tab, tab, tab, tab, _const_spec(vone.shape), _const_spec(dtb.shape),
    ]
    out_specs = [
        pl.BlockSpec((None, N_Q_HEADS, ts, HEAD_DIM), lambda bi, i: (bi, 0, i, 0)),
        pl.BlockSpec((None, N_KV_HEADS * HEAD_DIM, ts), lambda bi, i: (bi, 0, i)),
        pl.BlockSpec((None, ts, N_KV_HEADS * V_EXT), row),
        pl.BlockSpec((None, ts, CONV_DIM), row),
        pl.BlockSpec((None, ts, D_INNER), row),
        pl.BlockSpec((None, ts, LANES), row),
        pl.BlockSpec((None, ts, 2 * D_MODEL), row),
    ]
    out_shape = [
        jax.ShapeDtypeStruct((b, N_Q_HEADS, s, HEAD_DIM), BF16),
        jax.ShapeDtypeStruct((b, N_KV_HEADS * HEAD_DIM, s), BF16),
        jax.ShapeDtypeStruct((b, s, N_KV_HEADS * V_EXT), BF16),
        jax.ShapeDtypeStruct((b, s, CONV_DIM), BF16),
        jax.ShapeDtypeStruct((b, s, D_INNER), BF16),
        jax.ShapeDtypeStruct((b, s, LANES), F32),
        jax.ShapeDtypeStruct((b, s, 2 * D_MODEL), BF16),
    ]
    return pl.pallas_call(
        _inproj_kernel,
        grid=(b, n_t),
        in_specs=in_specs,
        out_specs=out_specs,
        out_shape=out_shape,
        compiler_params=pltpu.CompilerParams(dimension_semantics=("parallel", "parallel"),
                                             vmem_limit_bytes=VMEM_LIMIT),
        name="in_proj",
    )(x, shift1, scale1, norm1_w, wq, wk, wv, wx, wz, wdt, wg, gsum, qa, qb, ka, kb, vone, dtb)


def _attn_kernel(q_ref, kt_ref, v_ref, o_ref, *, tq, tk):
    s_len = kt_ref.shape[-1]
    rows = REP * tq
    q = q_ref[...].reshape(rows, HEAD_DIM)

    def body(j, carry):
        m, acc = carry
        off = pl.multiple_of(j * tk, tk)
        sc = _dot(q, kt_ref[:, pl.ds(off, tk)])
        m_new = jnp.maximum(m, jnp.max(sc, axis=-1, keepdims=True))
        p = jnp.exp2(sc - m_new)
        alpha = jnp.exp2(m - m_new)
        acc = alpha * acc + _dot(p.astype(BF16), v_ref[pl.ds(off, tk), :])
        return m_new, acc

    m0 = jnp.full((rows, 1), -jnp.inf, F32)
    acc0 = jnp.zeros((rows, V_EXT), F32)
    _, acc = lax.fori_loop(0, s_len // tk, body, (m0, acc0))
    o = acc[:, :HEAD_DIM] / acc[:, HEAD_DIM:HEAD_DIM + 1]
    for r in range(REP):
        o_ref[:, r * HEAD_DIM:(r + 1) * HEAD_DIM] = o[r * tq:(r + 1) * tq].astype(o_ref.dtype)


def _attention(q, kt, v, tq, tk):
    b, _, s, _ = q.shape
    return pl.pallas_call(
        functools.partial(_attn_kernel, tq=tq, tk=tk),
        grid=(b, N_KV_HEADS, s // tq),
        in_specs=[
            pl.BlockSpec((None, REP, tq, HEAD_DIM), lambda bi, g, i: (bi, g, i, 0)),
            pl.BlockSpec((None, HEAD_DIM, s), lambda bi, g, i: (bi, g, 0)),
            pl.BlockSpec((None, s, V_EXT), lambda bi, g, i: (bi, 0, g)),
        ],
        out_specs=pl.BlockSpec((None, tq, REP * HEAD_DIM), lambda bi, g, i: (bi, i, g)),
        out_shape=jax.ShapeDtypeStruct((b, s, N_Q_HEADS * HEAD_DIM), BF16),
        compiler_params=pltpu.CompilerParams(dimension_semantics=("parallel", "parallel", "arbitrary"),
                                             vmem_limit_bytes=VMEM_LIMIT),
        name="attn",
    )(q, kt, v)


def _expand_heads(v, e2):
    hi = v.astype(BF16)
    mid = (v - hi.astype(F32)).astype(BF16)
    return _dot(jnp.concatenate([hi, mid], axis=1), e2)


def _ssd_scan_chunk(xs, bmat, cmat, dt, a_row, tmat, e2, st_ref, y_ref, *, slot0, reverse):
    L = xs.shape[0]
    a = dt * a_row
    a_hi, a_mid, a_lo = _split3(a)
    cum = _dot(tmat, a_hi) + _dot(tmat, a_mid) + _dot(tmat, a_lo)
    cum_t = cum.T
    dt_t = dt.T
    total = cum[0:1, :] if reverse else cum[L - 1:L, :]
    w_exp = _expand_heads(jnp.exp(total - cum) * dt, e2)
    ecum_exp = _expand_heads(jnp.exp(cum), e2)
    dec_exp = _expand_heads(jnp.broadcast_to(jnp.exp(total), (8, LANES)), e2)[0:1, :]

    li = lax.broadcasted_iota(jnp.int32, (L, L), 0)
    si = lax.broadcasted_iota(jnp.int32, (L, L), 1)
    keep = (si >= li) if reverse else (li >= si)
    lane = lax.broadcasted_iota(jnp.int32, (L, LANES), 1)
    first_head = lane < SSD_P

    for g in range(SSD_G):
        gs = slice(g * SSD_N, (g + 1) * SSD_N)
        cols = slice(g * HEADS_PER_GROUP * SSD_P, (g + 1) * HEADS_PER_GROUP * SSD_P)
        bg = bmat[:, gs]
        cg_bf = cmat[:, gs].astype(BF16)
        cb = _dot_nt(cg_bf, bg.astype(BF16))
        st_g = st_ref[:, cols]
        y_off = _dot(cg_bf, st_g.astype(BF16)) * ecum_exp[:, cols]
        xw = (xs[:, cols] * w_exp[:, cols]).astype(BF16)
        st_ref[:, cols] = dec_exp[:, cols] * st_g + _dot(bg.T.astype(BF16), xw)
        for pr in range(HEADS_PER_GROUP // 2):
            h0 = g * HEADS_PER_GROUP + 2 * pr
            ms = []
            for hh in (h0, h0 + 1):
                j = slot0 + hh
                arg = cum[:, j:j + 1] - cum_t[j:j + 1, :]
                dm = jnp.exp(jnp.where(keep, arg, NEG_BIG)) * dt_t[j:j + 1, :]
                ms.append((cb * dm).astype(BF16))
            lhs = jnp.concatenate(ms, axis=1)
            pc = slice(h0 * SSD_P, h0 * SSD_P + LANES)
            xp = xs[:, pc]
            rhs = jnp.concatenate([jnp.where(first_head, xp, 0.0).astype(BF16),
                                   jnp.where(first_head, 0.0, xp).astype(BF16)], axis=0)
            y_ref[:, pc] = _dot(lhs, rhs) + y_off[:, pr * LANES:(pr + 1) * LANES]


def _ssd_fwd_kernel(x_ref, xprev_ref, xnext_ref, dt_ref, cw_ref, cb_ref, a_ref, dskip_ref, tmat_ref, e2_ref,
                    xc_out, y_out, ext_ref, st_ref, y_ref):
    c = pl.program_id(1)
    nc = pl.num_programs(1)
    L = x_ref.shape[0]

    @pl.when(c == 0)
    def _():
        st_ref[...] = jnp.zeros_like(st_ref)

    prev = xprev_ref[...].astype(F32)[CONV_HALO - 8:, :]
    nxt = xnext_ref[...].astype(F32)[:8, :]
    ext_ref[0:8, :] = jnp.where(c > 0, prev, 0.0)
    ext_ref[8:8 + L, :] = x_ref[...].astype(F32)
    ext_ref[8 + L:16 + L, :] = jnp.where(c < nc - 1, nxt, 0.0)

    pad = (D_CONV - 1) // 2
    acts = []
    for blk in range(CONV_DIM // 512):
        sl = slice(blk * 512, (blk + 1) * 512)
        acc = cb_ref[:, sl] + cw_ref[0:1, sl] * ext_ref[8 - pad:8 - pad + L, sl]
        for k in range(1, D_CONV):
            acc = acc + cw_ref[k:k + 1, sl] * ext_ref[8 - pad + k:8 - pad + k + L, sl]
        act = acc * jax.nn.sigmoid(acc)
        xc_out[:, sl] = act.astype(BF16)
        acts.append(act)
    xs = jnp.concatenate(acts[:D_INNER // 512], axis=1)
    bmat = acts[D_INNER // 512]
    cmat = acts[D_INNER // 512 + 1]

    _ssd_scan_chunk(xs, bmat, cmat, dt_ref[...], a_ref[...], tmat_ref[...], e2_ref[...], st_ref, y_ref,
                    slot0=0, reverse=False)
    y_out[...] = y_ref[...] + dskip_ref[...] * xs


def _ssd_bwd_kernel(xc_ref, dt_ref, yf_ref, zs_ref, a_ref, nw_ref, tmat_ref, e2_ref, o_ref, st_ref, y_ref):
    c = pl.program_id(1)

    @pl.when(c == 0)
    def _():
        st_ref[...] = jnp.zeros_like(st_ref)

    xc = xc_ref[...].astype(F32)
    xs = xc[:, :D_INNER]
    bmat = xc[:, D_INNER:D_INNER + SSD_G * SSD_N]
    cmat = xc[:, D_INNER + SSD_G * SSD_N:]
    _ssd_scan_chunk(xs, bmat, cmat, dt_ref[...], a_ref[...], tmat_ref[...], e2_ref[...], st_ref, y_ref,
                    slot0=SSD_H, reverse=True)
    y = y_ref[...] + yf_ref[...]
    gated = y * zs_ref[...].astype(F32)
    ms = jnp.mean(gated * gated, axis=-1, keepdims=True)
    o_ref[...] = (gated * lax.rsqrt(ms + EPS) * nw_ref[...]).astype(o_ref.dtype)


def _ssd(xbc, dt, zs, conv_w, conv_b, a_row, dskip, ssd_norm_w, tri_lo, tri_up, e2_f, e2_b):
    b, s, _ = xbc.shape
    nc = s // CHUNK
    hb = CHUNK // CONV_HALO
    n_halo = s // CONV_HALO
    row = lambda bi, c: (bi, c, 0)
    params = pltpu.CompilerParams(dimension_semantics=("parallel", "arbitrary"), vmem_limit_bytes=VMEM_LIMIT)
    xc, yf = pl.pallas_call(
        _ssd_fwd_kernel,
        grid=(b, nc),
        in_specs=[
            pl.BlockSpec((None, CHUNK, CONV_DIM), row),
            pl.BlockSpec((None, CONV_HALO, CONV_DIM), lambda bi, c: (bi, jnp.maximum(c * hb - 1, 0), 0)),
            pl.BlockSpec((None, CONV_HALO, CONV_DIM), lambda bi, c: (bi, jnp.minimum((c + 1) * hb, n_halo - 1), 0)),
            pl.BlockSpec((None, CHUNK, LANES), row),
            _const_spec(conv_w.shape), _const_spec(conv_b.shape), _const_spec(a_row.shape),
            _const_spec(dskip.shape), _const_spec(tri_lo.shape), _const_spec(e2_f.shape),
        ],
        out_specs=[pl.BlockSpec((None, CHUNK, CONV_DIM), row), pl.BlockSpec((None, CHUNK, D_INNER), row)],
        out_shape=[jax.ShapeDtypeStruct((b, s, CONV_DIM), BF16), jax.ShapeDtypeStruct((b, s, D_INNER), F32)],
        scratch_shapes=[pltpu.VMEM((CHUNK + 16, CONV_DIM), F32), pltpu.VMEM((SSD_N, D_INNER), F32),
                        pltpu.VMEM((CHUNK, D_INNER), F32)],
        compiler_params=params,
        name="ssd_fwd",
    )(xbc, xbc, xbc, dt, conv_w, conv_b, a_row, dskip, tri_lo, e2_f)

    rev = lambda bi, c: (bi, nc - 1 - c, 0)
    return pl.pallas_call(
        _ssd_bwd_kernel,
        grid=(b, nc),
        in_specs=[
            pl.BlockSpec((None, CHUNK, CONV_DIM), rev),
            pl.BlockSpec((None, CHUNK, LANES), rev),
            pl.BlockSpec((None, CHUNK, D_INNER), rev),
            pl.BlockSpec((None, CHUNK, D_INNER), rev),
            _const_spec(a_row.shape), _const_spec(ssd_norm_w.shape), _const_spec(tri_up.shape),
            _const_spec(e2_b.shape),
        ],
        out_specs=pl.BlockSpec((None, CHUNK, D_INNER), rev),
        out_shape=jax.ShapeDtypeStruct((b, s, D_INNER), BF16),
        scratch_shapes=[pltpu.VMEM((SSD_N, D_INNER), F32), pltpu.VMEM((CHUNK, D_INNER), F32)],
        compiler_params=params,
        name="ssd_bwd",
    )(xc, dt, yf, zs, a_row, ssd_norm_w, tri_up, e2_b)


def _merge_kernel(attn_ref, ssd_ref, g_ref, x_ref, gate1_ref, shift2_ref, scale2_ref, nw2_ref,
                  wao_ref, wso_ref, wo_ref, x1_out, h2_out):
    a = _dot(attn_ref[...], wao_ref[...])
    sm = _dot(ssd_ref[...], wso_ref[...])
    g = g_ref[...].astype(F32)
    merged = (g[:, :D_MODEL] * a + g[:, D_MODEL:] * sm).astype(BF16)
    x1 = x_ref[...] + gate1_ref[...] * _dot(merged, wo_ref[...])
    x1_out[...] = x1
    ms = jnp.mean(x1 * x1, axis=-1, keepdims=True)
    xn = x1 * lax.rsqrt(ms + EPS)
    h2_out[...] = (xn * (nw2_ref[...] * (1.0 + scale2_ref[...])) + shift2_ref[...]).astype(BF16)


def _merge(attn, ssd, gates, x, gate1, shift2, scale2, norm2_w, wao, wso, wo, tm):
    b, s, d = x.shape
    row = lambda bi, i: (bi, i, 0)
    per_b = pl.BlockSpec((None, 1, d), lambda bi, i: (bi, 0, 0))
    return pl.pallas_call(
        _merge_kernel,
        grid=(b, s // tm),
        in_specs=[
            pl.BlockSpec((None, tm, attn.shape[-1]), row), pl.BlockSpec((None, tm, D_INNER), row),
            pl.BlockSpec((None, tm, 2 * d), row), pl.BlockSpec((None, tm, d), row),
            per_b, per_b, per_b, _const_spec((1, d)),
            _const_spec(wao.shape), _const_spec(wso.shape), _const_spec(wo.shape),
        ],
        out_specs=[pl.BlockSpec((None, tm, d), row), pl.BlockSpec((None, tm, d), row)],
        out_shape=[jax.ShapeDtypeStruct((b, s, d), F32), jax.ShapeDtypeStruct((b, s, d), BF16)],
        compiler_params=pltpu.CompilerParams(dimension_semantics=("parallel", "parallel"),
                                             vmem_limit_bytes=VMEM_LIMIT),
        name="merge",
    )(attn, ssd, gates, x, gate1, shift2, scale2, norm2_w, wao, wso, wo)


def _mlp_kernel(h2_ref, x1_ref, gate2_ref, w1_ref, w2_ref, o_ref):
    h2 = h2_ref[...]
    acc = jnp.zeros(x1_ref.shape, F32)
    for blk in range(D_FF // 1024):
        sl = slice(blk * 1024, (blk + 1) * 1024)
        u = jnp.maximum(_dot(h2, w1_ref[:, sl]), 0.0)
        acc = acc + _dot((u * u).astype(BF16), w2_ref[sl, :])
    o_ref[...] = x1_ref[...] + gate2_ref[...] * acc


def _mlp(h2, x1, gate2, w1, w2, tm):
    b, s, d = x1.shape
    row = lambda bi, i: (bi, i, 0)
    return pl.pallas_call(
        _mlp_kernel,
        grid=(b, s // tm),
        in_specs=[
            pl.BlockSpec((None, tm, d), row), pl.BlockSpec((None, tm, d), row),
            pl.BlockSpec((None, 1, d), lambda bi, i: (bi, 0, 0)),
            _const_spec(w1.shape), _const_spec(w2.shape),
        ],
        out_specs=pl.BlockSpec((None, tm, d), row),
        out_shape=jax.ShapeDtypeStruct((b, s, d), F32),
        compiler_params=pltpu.CompilerParams(dimension_semantics=("parallel", "parallel"),
                                             vmem_limit_bytes=VMEM_LIMIT),
        name="mlp",
    )(h2, x1, gate2, w1, w2)


def _swap_halves(w):
    return w.reshape(2, 2, 16)[:, ::-1, :].reshape(HEAD_DIM)


def _rope_tables(s, norm_w, scale):
    pos = jnp.arange(s, dtype=jnp.int32)
    pos_row = (pos // GRID_W).astype(F32)
    pos_col = (pos % GRID_W).astype(F32)
    axis_dim = HEAD_DIM // 2
    inv_freq = ROPE_THETA ** (-jnp.arange(0, axis_dim, 2, dtype=F32) / axis_dim)
    ang_r = pos_row[:, None] * inv_freq[None, :]
    ang_c = pos_col[:, None] * inv_freq[None, :]
    cos = jnp.concatenate([jnp.cos(ang_r)] * 2 + [jnp.cos(ang_c)] * 2, axis=-1)
    sin_r, sin_c = jnp.sin(ang_r), jnp.sin(ang_c)
    sin = jnp.concatenate([-sin_r, sin_r, -sin_c, sin_c], axis=-1)
    ta = cos * norm_w[None, :] * scale
    tb = sin * _swap_halves(norm_w)[None, :] * scale
    return jnp.tile(ta, (1, 2)), jnp.tile(tb, (1, 2))


def _head_expansion(slot0):
    e = np.zeros((LANES, D_INNER), np.float32)
    for h in range(SSD_H):
        e[slot0 + h, h * SSD_P:(h + 1) * SSD_P] = 1.0
    return jnp.asarray(np.concatenate([e, e], axis=0), BF16)


def _pick_tile(s, pref):
    t = min(pref, s)
    assert s % t == 0
    return t


def kernel(x, c, w_ada, b_ada, norm1_w, norm2_w, w_in, q_norm_w, k_norm_w, conv_w, conv_b, A_log, dt_bias,
           ssd_D, ssd_norm_w, w_attn_out, w_ssd_out, w_o, w_mlp1, w_mlp2):
    b, s, d = x.shape
    depth = w_ada.shape[0]
    assert d == D_MODEL and s % CHUNK == 0 and s % GRID_W == 0

    qdim = N_Q_HEADS * HEAD_DIM
    kvdim = N_KV_HEADS * HEAD_DIM
    o_k, o_v, o_x = qdim, qdim + kvdim, qdim + 2 * kvdim
    o_z = o_x + CONV_DIM
    o_dt = o_z + D_INNER
    o_g = o_dt + 2 * SSD_H

    gsum = jnp.asarray(np.kron(np.eye(4, dtype=np.float32), np.full((HEAD_DIM, HEAD_DIM), 1.0 / HEAD_DIM,
                                                                       np.float32)), BF16)
    vone = jnp.zeros((N_KV_HEADS, V_EXT), F32).at[:, HEAD_DIM].set(1.0).reshape(1, N_KV_HEADS * V_EXT)
    tri = np.tril(np.ones((CHUNK, CHUNK), np.float32))
    tri_lo = jnp.asarray(tri, BF16)
    tri_up = jnp.asarray(tri.T, BF16)
    e2_f = _head_expansion(0)
    e2_b = _head_expansion(SSD_H)
    ts = _pick_tile(s, 512)
    tq = _pick_tile(s, 128)
    tk = _pick_tile(s, 512)

    for l in range(depth):
        mod = _ada(c, w_ada[l], b_ada[l])
        shift1, scale1, gate1, shift2, scale2, gate2 = [m[:, None, :] for m in jnp.split(mod, 6, axis=-1)]

        w = w_in[l]
        wq = w[:, :o_k].astype(BF16)
        wk = w[:, o_k:o_v].astype(BF16)
        wv = jnp.pad(w[:, o_v:o_x].reshape(d, N_KV_HEADS, HEAD_DIM),
                     ((0, 0), (0, 0), (0, V_EXT - HEAD_DIM))).reshape(d, N_KV_HEADS * V_EXT).astype(BF16)
        wx = w[:, o_x:o_z].astype(BF16)
        wz = w[:, o_z:o_dt].astype(BF16)
        wdt = jnp.pad(w[:, o_dt:o_g], ((0, 0), (0, LANES - 2 * SSD_H))).astype(BF16)
        wg = w[:, o_g:].astype(BF16)
        dtb = jnp.pad(dt_bias[l].reshape(1, 2 * SSD_H), ((0, 0), (0, LANES - 2 * SSD_H)))
        qa, qb = _rope_tables(s, q_norm_w[l], math.log2(math.e) * HEAD_DIM ** -0.5)
        ka, kb = _rope_tables(s, k_norm_w[l], 1.0)

        q, kt, v, xbc, zs, dt, gates = _inproj(x, shift1, scale1, norm1_w[l].reshape(1, d), wq, wk, wv, wx, wz,
                                                wdt, wg, gsum, qa, qb, ka, kb, vone, dtb, ts)
        attn = _attention(q, kt, v, tq, tk)

        a_row = jnp.pad(-jnp.exp(A_log[l].astype(F32)).reshape(1, 2 * SSD_H), ((0, 0), (0, LANES - 2 * SSD_H)))
        dskip = jnp.repeat(ssd_D[l].astype(F32), SSD_P).reshape(1, D_INNER)
        ssd = _ssd(xbc, dt, zs, conv_w[l], conv_b[l].reshape(1, CONV_DIM), a_row, dskip,
                   ssd_norm_w[l].reshape(1, D_INNER), tri_lo, tri_up, e2_f, e2_b)

        x1, h2 = _merge(attn, ssd, gates, x, gate1, shift2, scale2, norm2_w[l].reshape(1, d),
                        w_attn_out[l].astype(BF16), w_ssd_out[l].astype(BF16), w_o[l].astype(BF16), ts)
        x = _mlp(h2, x1, gate2, w_mlp1[l].astype(BF16), w_mlp2[l].astype(BF16), ts)
    return x
```

```python
import functools
import math

import jax
import jax.numpy as jnp
import numpy as np
from jax import lax
from jax.experimental import pallas as pl
from jax.experimental.pallas import tpu as pltpu

F32 = jnp.float32
BF16 = jnp.bfloat16

D_MODEL = 1024
GRID_W = 64
N_Q_HEADS = 16
N_KV_HEADS = 4
REP = N_Q_HEADS // N_KV_HEADS
HEAD_DIM = 64
ROPE_THETA = 10000.0
D_INNER = 2048
SSD_P = 64
SSD_H = D_INNER // SSD_P
SSD_G = 4
SSD_N = 128
HEADS_PER_GROUP = SSD_H // SSD_G
D_CONV = 5
CHUNK = 128
CONV_DIM = D_INNER + 2 * SSD_G * SSD_N
D_FF = 4 * D_MODEL
EPS = 1e-6

LANES = 128
V_EXT = 128
CONV_HALO = 16
NEG_BIG = -1e30
MAX_UNSHIFTED_SCORE = 60.0
VMEM_LIMIT = 56 * 1024 * 1024


def _dot(a, b):
    return jnp.dot(a, b, preferred_element_type=F32)


def _dot_nt(a, b):
    return lax.dot_general(a, b, (((1,), (1,)), ((), ())), preferred_element_type=F32)


def _split3(v):
    hi = v.astype(BF16)
    r = v - hi.astype(F32)
    mid = r.astype(BF16)
    lo = (r - mid.astype(F32)).astype(BF16)
    return hi, mid, lo


def _const_spec(shape):
    return pl.BlockSpec(shape, lambda *_: (0,) * len(shape))


def _ada_kernel(c_ref, w_ref, b_ref, o_ref):
    c = c_ref[...]
    sc = c * jax.nn.sigmoid(c)
    a_hi, a_mid, a_lo = _split3(sc)
    w_hi, w_mid, w_lo = _split3(w_ref[...])
    acc = _dot(a_hi, w_hi)
    acc += _dot(a_hi, w_mid) + _dot(a_mid, w_hi)
    acc += _dot(a_hi, w_lo) + _dot(a_mid, w_mid) + _dot(a_lo, w_hi)
    o_ref[...] = acc + b_ref[...]


def _ada(c, w_ada, b_ada):
    b, d = c.shape
    n = w_ada.shape[1]
    tn = 1536
    return pl.pallas_call(
        _ada_kernel,
        grid=(n // tn,),
        in_specs=[
            pl.BlockSpec((b, d), lambda j: (0, 0)),
            pl.BlockSpec((d, tn), lambda j: (0, j)),
            pl.BlockSpec((1, tn), lambda j: (0, j)),
        ],
        out_specs=pl.BlockSpec((b, tn), lambda j: (0, j)),
        out_shape=jax.ShapeDtypeStruct((b, n), F32),
        compiler_params=pltpu.CompilerParams(dimension_semantics=("arbitrary",), vmem_limit_bytes=VMEM_LIMIT),
        name="ada",
    )(c, w_ada, b_ada.reshape(1, n))


def _norm_rope_block(t, ss, ta, tb, swap_lo):
    n = t * lax.rsqrt(ss + EPS)
    fwd = pltpu.roll(n, LANES - 16, axis=1)
    bwd = pltpu.roll(n, 16, axis=1)
    sw = jnp.where(swap_lo, fwd, bwd)
    return n * ta + sw * tb


def _inproj_kernel(x_ref, shift_ref, scale_ref, nw_ref, wq_ref, wk_ref, wv_ref, wx_ref, wz_ref, wdt_ref, wg_ref,
                   gsum_ref, qa_ref, qb_ref, ka_ref, kb_ref, vone_ref, dtb_ref,
                   q_out, kt_out, v_out, xbc_out, z_out, dt_out, g_out):
    x = x_ref[...]
    ts = x.shape[0]
    ms = jnp.mean(x * x, axis=-1, keepdims=True)
    xn = x * lax.rsqrt(ms + EPS)
    h = (xn * (nw_ref[...] * (1.0 + scale_ref[...])) + shift_ref[...]).astype(BF16)

    lane = lax.broadcasted_iota(jnp.int32, (ts, LANES), 1)
    swap_lo = (lane & 31) < 16
    gsum = gsum_ref[...]

    qf = _dot(h, wq_ref[...])
    qa = qa_ref[...]
    qb = qb_ref[...]
    for blk in range(N_Q_HEADS * HEAD_DIM // 256):
        t = qf[:, blk * 256:(blk + 1) * 256]
        ss = _dot((t * t).astype(BF16), gsum)
        for half in range(2):
            sl = slice(half * LANES, (half + 1) * LANES)
            r = _norm_rope_block(t[:, sl], ss[:, sl], qa, qb, swap_lo).astype(BF16)
            hd = blk * 4 + half * 2
            q_out[hd] = r[:, :HEAD_DIM]
            q_out[hd + 1] = r[:, HEAD_DIM:]

    kf = _dot(h, wk_ref[...])
    ssk = _dot((kf * kf).astype(BF16), gsum)
    ka = ka_ref[...]
    kb = kb_ref[...]
    for half in range(2):
        sl = slice(half * LANES, (half + 1) * LANES)
        r = _norm_rope_block(kf[:, sl], ssk[:, sl], ka, kb, swap_lo)
        kt_out[sl, :] = r.T.astype(BF16)

    v_out[...] = (_dot(h, wv_ref[...]) + vone_ref[...]).astype(BF16)

    for blk in range(CONV_DIM // 512):
        sl = slice(blk * 512, (blk + 1) * 512)
        xbc_out[:, sl] = _dot(h, wx_ref[:, sl]).astype(BF16)

    for blk in range(D_INNER // 512):
        sl = slice(blk * 512, (blk + 1) * 512)
        zf = _dot(h, wz_ref[:, sl])
        z_out[:, sl] = (zf * jax.nn.sigmoid(zf)).astype(BF16)

    dtf = _dot(h, wdt_ref[...]) + dtb_ref[...]
    dt_out[...] = jnp.maximum(dtf, 0.0) + jnp.log1p(jnp.exp(-jnp.abs(dtf)))

    for blk in range(2 * D_MODEL // 512):
        sl = slice(blk * 512, (blk + 1) * 512)
        g_out[:, sl] = jax.nn.sigmoid(_dot(h, wg_ref[:, sl])).astype(BF16)


def _inproj(x, shift1, scale1, norm1_w, wq, wk, wv, wx, wz, wdt, wg, gsum, qa, qb, ka, kb, vone, dtb, ts):
    b, s, d = x.shape
    n_t = s // ts
    row = lambda bi, i: (bi, i, 0)
    per_b = pl.BlockSpec((None, 1, d), lambda bi, i: (bi, 0, 0))
    tab = pl.BlockSpec((ts, LANES), lambda bi, i: (i, 0))
    in_specs = [
        pl.BlockSpec((None, ts, d), row), per_b, per_b, _const_spec((1, d)),
        _const_spec(wq.shape), _const_spec(wk.shape), _const_spec(wv.shape), _const_spec(wx.shape),
        _const_spec(wz.shape), _const_spec(wdt.shape), _const_spec(wg.shape), _const_spec(gsum.shape),
        tab, tab, tab, tab, _const_spec(vone.shape), _const_spec(dtb.shape),
    ]
    out_specs = [
        pl.BlockSpec((None, N_Q_HEADS, ts, HEAD_DIM), lambda bi, i: (bi, 0, i, 0)),
        pl.BlockSpec((None, N_KV_HEADS * HEAD_DIM, ts), lambda bi, i: (bi, 0, i)),
        pl.BlockSpec((None, ts, N_KV_HEADS * V_EXT), row),
        pl.BlockSpec((None, ts, CONV_DIM), row),
        pl.BlockSpec((None, ts, D_INNER), row),
        pl.BlockSpec((None, ts, LANES), row),
        pl.BlockSpec((None, ts, 2 * D_MODEL), row),
    ]
    out_shape = [
        jax.ShapeDtypeStruct((b, N_Q_HEADS, s, HEAD_DIM), BF16),
        jax.ShapeDtypeStruct((b, N_KV_HEADS * HEAD_DIM, s), BF16),
        jax.ShapeDtypeStruct((b, s, N_KV_HEADS * V_EXT), BF16),
        jax.ShapeDtypeStruct((b, s, CONV_DIM), BF16),
        jax.ShapeDtypeStruct((b, s, D_INNER), BF16),
        jax.ShapeDtypeStruct((b, s, LANES), F32),
        jax.ShapeDtypeStruct((b, s, 2 * D_MODEL), BF16),
    ]
    return pl.pallas_call(
        _inproj_kernel,
        grid=(b, n_t),
        in_specs=in_specs,
        out_specs=out_specs,
        out_shape=out_shape,
        compiler_params=pltpu.CompilerParams(dimension_semantics=("parallel", "parallel"),
                                             vmem_limit_bytes=VMEM_LIMIT),
        name="in_proj",
    )(x, shift1, scale1, norm1_w, wq, wk, wv, wx, wz, wdt, wg, gsum, qa, qb, ka, kb, vone, dtb)


def _attn_kernel(q_ref, kt_ref, v_ref, o_ref, *, tq, tk):
    s_len = kt_ref.shape[-1]
    rows = REP * tq
    q = q_ref[...].reshape(rows, HEAD_DIM)

    def body(j, carry):
        m, acc = carry
        off = pl.multiple_of(j * tk, tk)
        sc = _dot(q, kt_ref[:, pl.ds(off, tk)])
        m_new = jnp.maximum(m, jnp.max(sc, axis=-1, keepdims=True))
        p = jnp.exp2(sc - m_new)
        alpha = jnp.exp2(m - m_new)
        acc = alpha * acc + _dot(p.astype(BF16), v_ref[pl.ds(off, tk), :])
        return m_new, acc

    m0 = jnp.full((rows, 1), -jnp.inf, F32)
    acc0 = jnp.zeros((rows, V_EXT), F32)
    _, acc = lax.fori_loop(0, s_len // tk, body, (m0, acc0))
    o = acc[:, :HEAD_DIM] / acc[:, HEAD_DIM:HEAD_DIM + 1]
    for r in range(REP):
        o_ref[:, r * HEAD_DIM:(r + 1) * HEAD_DIM] = o[r * tq:(r + 1) * tq].astype(o_ref.dtype)


def _attn_unshifted_kernel(q_ref, kt_ref, v_ref, o_ref, *, tq, tk):
    s_len = kt_ref.shape[-1]
    nk = s_len // tk
    rows = REP * tq
    q = q_ref[...].reshape(rows, HEAD_DIM)

    def probs(j):
        off = pl.multiple_of(j * tk, tk)
        return jnp.exp2(_dot(q, kt_ref[:, pl.ds(off, tk)])).astype(BF16)

    def pv(p, j):
        off = pl.multiple_of(j * tk, tk)
        return _dot(p, v_ref[pl.ds(off, tk), :])

    def body(j, carry):
        p, acc = carry
        p_next = probs(j + 1)
        return p_next, acc + pv(p, j)

    p, acc = lax.fori_loop(0, nk - 1, body, (probs(0), jnp.zeros((rows, V_EXT), F32)), unroll=True)
    acc = acc + pv(p, nk - 1)
    o = acc[:, :HEAD_DIM] / acc[:, HEAD_DIM:HEAD_DIM + 1]
    for r in range(REP):
        o_ref[:, r * HEAD_DIM:(r + 1) * HEAD_DIM] = o[r * tq:(r + 1) * tq].astype(o_ref.dtype)


def _attention(q, kt, v, tq, tk, body):
    b, _, s, _ = q.shape
    return pl.pallas_call(
        functools.partial(body, tq=tq, tk=tk),
        grid=(b, N_KV_HEADS, s // tq),
        in_specs=[
            pl.BlockSpec((None, REP, tq, HEAD_DIM), lambda bi, g, i: (bi, g, i, 0)),
            pl.BlockSpec((None, HEAD_DIM, s), lambda bi, g, i: (bi, g, 0)),
            pl.BlockSpec((None, s, V_EXT), lambda bi, g, i: (bi, 0, g)),
        ],
        out_specs=pl.BlockSpec((None, tq, REP * HEAD_DIM), lambda bi, g, i: (bi, i, g)),
        out_shape=jax.ShapeDtypeStruct((b, s, N_Q_HEADS * HEAD_DIM), BF16),
        compiler_params=pltpu.CompilerParams(dimension_semantics=("parallel", "parallel", "arbitrary"),
                                             vmem_limit_bytes=VMEM_LIMIT),
        name="attn_online" if body is _attn_kernel else "attn_unshifted",
    )(q, kt, v)


def _expand_heads(v, e2):
    hi = v.astype(BF16)
    mid = (v - hi.astype(F32)).astype(BF16)
    return _dot(jnp.concatenate([hi, mid], axis=1), e2)


def _ssd_scan_chunk(xs, bmat, cmat, dt, a_row, tmat, e2, st_ref, y_ref, *, slot0, reverse):
    L = xs.shape[0]
    a = dt * a_row
    a_hi, a_mid, a_lo = _split3(a)
    cum = _dot(tmat, a_hi) + _dot(tmat, a_mid) + _dot(tmat, a_lo)
    cum_t = cum.T
    dt_t = dt.T
    total = cum[0:1, :] if reverse else cum[L - 1:L, :]
    w_exp = _expand_heads(jnp.exp(total - cum) * dt, e2)
    ecum_exp = _expand_heads(jnp.exp(cum), e2)
    dec_exp = _expand_heads(jnp.broadcast_to(jnp.exp(total), (8, LANES)), e2)[0:1, :]

    li = lax.broadcasted_iota(jnp.int32, (L, L), 0)
    si = lax.broadcasted_iota(jnp.int32, (L, L), 1)
    keep = (si >= li) if reverse else (li >= si)
    lane = lax.broadcasted_iota(jnp.int32, (L, LANES), 1)
    first_head = lane < SSD_P

    for g in range(SSD_G):
        gs = slice(g * SSD_N, (g + 1) * SSD_N)
        cols = slice(g * HEADS_PER_GROUP * SSD_P, (g + 1) * HEADS_PER_GROUP * SSD_P)
        bg = bmat[:, gs]
        cg_bf = cmat[:, gs].astype(BF16)
        cb = _dot_nt(cg_bf, bg.astype(BF16))
        st_g = st_ref[:, cols]
        y_off = _dot(cg_bf, st_g.astype(BF16)) * ecum_exp[:, cols]
        xw = (xs[:, cols] * w_exp[:, cols]).astype(BF16)
        st_ref[:, cols] = dec_exp[:, cols] * st_g + _dot(bg.T.astype(BF16), xw)
        for pr in range(HEADS_PER_GROUP // 2):
            h0 = g * HEADS_PER_GROUP + 2 * pr
            ms = []
            for hh in (h0, h0 + 1):
                j = slot0 + hh
                arg = cum[:, j:j + 1] - cum_t[j:j + 1, :]
                dm = jnp.exp(jnp.where(keep, arg, NEG_BIG)) * dt_t[j:j + 1, :]
                ms.append((cb * dm).astype(BF16))
            lhs = jnp.concatenate(ms, axis=1)
            pc = slice(h0 * SSD_P, h0 * SSD_P + LANES)
            xp = xs[:, pc]
            rhs = jnp.concatenate([jnp.where(first_head, xp, 0.0).astype(BF16),
                                   jnp.where(first_head, 0.0, xp).astype(BF16)], axis=0)
            y_ref[:, pc] = _dot(lhs, rhs) + y_off[:, pr * LANES:(pr + 1) * LANES]


def _ssd_fwd_kernel(x_ref, xprev_ref, xnext_ref, dt_ref, cw_ref, cb_ref, a_ref, dskip_ref, tmat_ref, e2_ref,
                    xc_out, y_out, ext_ref, st_ref, y_ref):
    c = pl.program_id(1)
    nc = pl.num_programs(1)
    L = x_ref.shape[0]

    @pl.when(c == 0)
    def _():
        st_ref[...] = jnp.zeros_like(st_ref)

    prev = xprev_ref[...].astype(F32)[CONV_HALO - 8:, :]
    nxt = xnext_ref[...].astype(F32)[:8, :]
    ext_ref[0:8, :] = jnp.where(c > 0, prev, 0.0)
    ext_ref[8:8 + L, :] = x_ref[...].astype(F32)
    ext_ref[8 + L:16 + L, :] = jnp.where(c < nc - 1, nxt, 0.0)

    pad = (D_CONV - 1) // 2
    acts = []
    for blk in range(CONV_DIM // 512):
        sl = slice(blk * 512, (blk + 1) * 512)
        acc = cb_ref[:, sl] + cw_ref[0:1, sl] * ext_ref[8 - pad:8 - pad + L, sl]
        for k in range(1, D_CONV):
            acc = acc + cw_ref[k:k + 1, sl] * ext_ref[8 - pad + k:8 - pad + k + L, sl]
        act = acc * jax.nn.sigmoid(acc)
        xc_out[:, sl] = act.astype(BF16)
        acts.append(act)
    xs = jnp.concatenate(acts[:D_INNER // 512], axis=1)
    bmat = acts[D_INNER // 512]
    cmat = acts[D_INNER // 512 + 1]

    _ssd_scan_chunk(xs, bmat, cmat, dt_ref[...], a_ref[...], tmat_ref[...], e2_ref[...], st_ref, y_ref,
                    slot0=0, reverse=False)
    y_out[...] = y_ref[...] + dskip_ref[...] * xs


def _ssd_bwd_kernel(xc_ref, dt_ref, yf_ref, zs_ref, a_ref, nw_ref, tmat_ref, e2_ref, o_ref, st_ref, y_ref):
    c = pl.program_id(1)

    @pl.when(c == 0)
    def _():
        st_ref[...] = jnp.zeros_like(st_ref)

    xc = xc_ref[...].astype(F32)
    xs = xc[:, :D_INNER]
    bmat = xc[:, D_INNER:D_INNER + SSD_G * SSD_N]
    cmat = xc[:, D_INNER + SSD_G * SSD_N:]
    _ssd_scan_chunk(xs, bmat, cmat, dt_ref[...], a_ref[...], tmat_ref[...], e2_ref[...], st_ref, y_ref,
                    slot0=SSD_H, reverse=True)
    y = y_ref[...] + yf_ref[...]
    gated = y * zs_ref[...].astype(F32)
    ms = jnp.mean(gated * gated, axis=-1, keepdims=True)
    o_ref[...] = (gated * lax.rsqrt(ms + EPS) * nw_ref[...]).astype(o_ref.dtype)


def _ssd(xbc, dt, zs, conv_w, conv_b, a_row, dskip, ssd_norm_w, tri_lo, tri_up, e2_f, e2_b):
    b, s, _ = xbc.shape
    nc = s // CHUNK
    hb = CHUNK // CONV_HALO
    n_halo = s // CONV_HALO
    row = lambda bi, c: (bi, c, 0)
    params = pltpu.CompilerParams(dimension_semantics=("parallel", "arbitrary"), vmem_limit_bytes=VMEM_LIMIT)
    xc, yf = pl.pallas_call(
        _ssd_fwd_kernel,
        grid=(b, nc),
        in_specs=[
            pl.BlockSpec((None, CHUNK, CONV_DIM), row),
            pl.BlockSpec((None, CONV_HALO, CONV_DIM), lambda bi, c: (bi, jnp.maximum(c * hb - 1, 0), 0)),
            pl.BlockSpec((None, CONV_HALO, CONV_DIM), lambda bi, c: (bi, jnp.minimum((c + 1) * hb, n_halo - 1), 0)),
            pl.BlockSpec((None, CHUNK, LANES), row),
            _const_spec(conv_w.shape), _const_spec(conv_b.shape), _const_spec(a_row.shape),
            _const_spec(dskip.shape), _const_spec(tri_lo.shape), _const_spec(e2_f.shape),
        ],
        out_specs=[pl.BlockSpec((None, CHUNK, CONV_DIM), row), pl.BlockSpec((None, CHUNK, D_INNER), row)],
        out_shape=[jax.ShapeDtypeStruct((b, s, CONV_DIM), BF16), jax.ShapeDtypeStruct((b, s, D_INNER), F32)],
        scratch_shapes=[pltpu.VMEM((CHUNK + 16, CONV_DIM), F32), pltpu.VMEM((SSD_N, D_INNER), F32),
                        pltpu.VMEM((CHUNK, D_INNER), F32)],
        compiler_params=params,
        name="ssd_fwd",
    )(xbc, xbc, xbc, dt, conv_w, conv_b, a_row, dskip, tri_lo, e2_f)

    rev = lambda bi, c: (bi, nc - 1 - c, 0)
    return pl.pallas_call(
        _ssd_bwd_kernel,
        grid=(b, nc),
        in_specs=[
            pl.BlockSpec((None, CHUNK, CONV_DIM), rev),
            pl.BlockSpec((None, CHUNK, LANES), rev),
            pl.BlockSpec((None, CHUNK, D_INNER), rev),
            pl.BlockSpec((None, CHUNK, D_INNER), rev),
            _const_spec(a_row.shape), _const_spec(ssd_norm_w.shape), _const_spec(tri_up.shape),
            _const_spec(e2_b.shape),
        ],
        out_specs=pl.BlockSpec((None, CHUNK, D_INNER), rev),
        out_shape=jax.ShapeDtypeStruct((b, s, D_INNER), BF16),
        scratch_shapes=[pltpu.VMEM((SSD_N, D_INNER), F32), pltpu.VMEM((CHUNK, D_INNER), F32)],
        compiler_params=params,
        name="ssd_bwd",
    )(xc, dt, yf, zs, a_row, ssd_norm_w, tri_up, e2_b)


def _merge_kernel(attn_ref, ssd_ref, g_ref, x_ref, gate1_ref, shift2_ref, scale2_ref, nw2_ref,
                  wao_ref, wso_ref, wo_ref, x1_out, h2_out):
    a = _dot(attn_ref[...], wao_ref[...])
    sm = _dot(ssd_ref[...], wso_ref[...])
    g = g_ref[...].astype(F32)
    merged = (g[:, :D_MODEL] * a + g[:, D_MODEL:] * sm).astype(BF16)
    x1 = x_ref[...] + gate1_ref[...] * _dot(merged, wo_ref[...])
    x1_out[...] = x1
    ms = jnp.mean(x1 * x1, axis=-1, keepdims=True)
    xn = x1 * lax.rsqrt(ms + EPS)
    h2_out[...] = (xn * (nw2_ref[...] * (1.0 + scale2_ref[...])) + shift2_ref[...]).astype(BF16)


def _merge(attn, ssd, gates, x, gate1, shift2, scale2, norm2_w, wao, wso, wo, tm):
    b, s, d = x.shape
    row = lambda bi, i: (bi, i, 0)
    per_b = pl.BlockSpec((None, 1, d), lambda bi, i: (bi, 0, 0))
    return pl.pallas_call(
        _merge_kernel,
        grid=(b, s // tm),
        in_specs=[
            pl.BlockSpec((None, tm, attn.shape[-1]), row), pl.BlockSpec((None, tm, D_INNER), row),
            pl.BlockSpec((None, tm, 2 * d), row), pl.BlockSpec((None, tm, d), row),
            per_b, per_b, per_b, _const_spec((1, d)),
            _const_spec(wao.shape), _const_spec(wso.shape), _const_spec(wo.shape),
        ],
        out_specs=[pl.BlockSpec((None, tm, d), row), pl.BlockSpec((None, tm, d), row)],
        out_shape=[jax.ShapeDtypeStruct((b, s, d), F32), jax.ShapeDtypeStruct((b, s, d), BF16)],
        compiler_params=pltpu.CompilerParams(dimension_semantics=("parallel", "parallel"),
                                             vmem_limit_bytes=VMEM_LIMIT),
        name="merge",
    )(attn, ssd, gates, x, gate1, shift2, scale2, norm2_w, wao, wso, wo)


def _mlp_kernel(h2_ref, x1_ref, gate2_ref, w1_ref, w2_ref, o_ref):
    h2 = h2_ref[...]
    acc = jnp.zeros(x1_ref.shape, F32)
    for blk in range(D_FF // 1024):
        sl = slice(blk * 1024, (blk + 1) * 1024)
        u = jnp.maximum(_dot(h2, w1_ref[:, sl]), 0.0)
        acc = acc + _dot((u * u).astype(BF16), w2_ref[sl, :])
    o_ref[...] = x1_ref[...] + gate2_ref[...] * acc


def _mlp(h2, x1, gate2, w1, w2, tm):
    b, s, d = x1.shape
    row = lambda bi, i: (bi, i, 0)
    return pl.pallas_call(
        _mlp_kernel,
        grid=(b, s // tm),
        in_specs=[
            pl.BlockSpec((None, tm, d), row), pl.BlockSpec((None, tm, d), row),
            pl.BlockSpec((None, 1, d), lambda bi, i: (bi, 0, 0)),
            _const_spec(w1.shape), _const_spec(w2.shape),
        ],
        out_specs=pl.BlockSpec((None, tm, d), row),
        out_shape=jax.ShapeDtypeStruct((b, s, d), F32),
        compiler_params=pltpu.CompilerParams(dimension_semantics=("parallel", "parallel"),
                                             vmem_limit_bytes=VMEM_LIMIT),
        name="mlp",
    )(h2, x1, gate2, w1, w2)


def _swap_halves(w):
    return w.reshape(2, 2, 16)[:, ::-1, :].reshape(HEAD_DIM)


def _rope_tables(s, norm_w, scale):
    pos = jnp.arange(s, dtype=jnp.int32)
    pos_row = (pos // GRID_W).astype(F32)
    pos_col = (pos % GRID_W).astype(F32)
    axis_dim = HEAD_DIM // 2
    inv_freq = ROPE_THETA ** (-jnp.arange(0, axis_dim, 2, dtype=F32) / axis_dim)
    ang_r = pos_row[:, None] * inv_freq[None, :]
    ang_c = pos_col[:, None] * inv_freq[None, :]
    cos = jnp.concatenate([jnp.cos(ang_r)] * 2 + [jnp.cos(ang_c)] * 2, axis=-1)
    sin_r, sin_c = jnp.sin(ang_r), jnp.sin(ang_c)
    sin = jnp.concatenate([-sin_r, sin_r, -sin_c, sin_c], axis=-1)
    ta = cos * norm_w[None, :] * scale
    tb = sin * _swap_halves(norm_w)[None, :] * scale
    return jnp.tile(ta, (1, 2)), jnp.tile(tb, (1, 2))


def _head_expansion(slot0):
    e = np.zeros((LANES, D_INNER), np.float32)
    for h in range(SSD_H):
        e[slot0 + h, h * SSD_P:(h + 1) * SSD_P] = 1.0
    return jnp.asarray(np.concatenate([e, e], axis=0), BF16)


def _pick_tile(s, pref):
    t = min(pref, s)
    assert s % t == 0
    return t


def kernel(x, c, w_ada, b_ada, norm1_w, norm2_w, w_in, q_norm_w, k_norm_w, conv_w, conv_b, A_log, dt_bias,
           ssd_D, ssd_norm_w, w_attn_out, w_ssd_out, w_o, w_mlp1, w_mlp2):
    b, s, d = x.shape
    depth = w_ada.shape[0]
    assert d == D_MODEL and s % CHUNK == 0 and s % GRID_W == 0

    qdim = N_Q_HEADS * HEAD_DIM
    kvdim = N_KV_HEADS * HEAD_DIM
    o_k, o_v, o_x = qdim, qdim + kvdim, qdim + 2 * kvdim
    o_z = o_x + CONV_DIM
    o_dt = o_z + D_INNER
    o_g = o_dt + 2 * SSD_H

    gsum = jnp.asarray(np.kron(np.eye(4, dtype=np.float32), np.full((HEAD_DIM, HEAD_DIM), 1.0 / HEAD_DIM,
                                                                       np.float32)), BF16)
    vone = jnp.zeros((N_KV_HEADS, V_EXT), F32).at[:, HEAD_DIM].set(1.0).reshape(1, N_KV_HEADS * V_EXT)
    tri = np.tril(np.ones((CHUNK, CHUNK), np.float32))
    tri_lo = jnp.asarray(tri, BF16)
    tri_up = jnp.asarray(tri.T, BF16)
    e2_f = _head_expansion(0)
    e2_b = _head_expansion(SSD_H)
    ts = _pick_tile(s, 512)
    tq = _pick_tile(s, 128)
    tk = _pick_tile(s, 512)

    for l in range(depth):
        mod = _ada(c, w_ada[l], b_ada[l])
        shift1, scale1, gate1, shift2, scale2, gate2 = [m[:, None, :] for m in jnp.split(mod, 6, axis=-1)]

        w = w_in[l]
        wq = w[:, :o_k].astype(BF16)
        wk = w[:, o_k:o_v].astype(BF16)
        wv = jnp.pad(w[:, o_v:o_x].reshape(d, N_KV_HEADS, HEAD_DIM),
                     ((0, 0), (0, 0), (0, V_EXT - HEAD_DIM))).reshape(d, N_KV_HEADS * V_EXT).astype(BF16)
        wx = w[:, o_x:o_z].astype(BF16)
        wz = w[:, o_z:o_dt].astype(BF16)
        wdt = jnp.pad(w[:, o_dt:o_g], ((0, 0), (0, LANES - 2 * SSD_H))).astype(BF16)
        wg = w[:, o_g:].astype(BF16)
        dtb = jnp.pad(dt_bias[l].reshape(1, 2 * SSD_H), ((0, 0), (0, LANES - 2 * SSD_H)))
        q_scale = math.log2(math.e) * HEAD_DIM ** -0.5
        qa, qb = _rope_tables(s, q_norm_w[l], q_scale)
        ka, kb = _rope_tables(s, k_norm_w[l], 1.0)

        q, kt, v, xbc, zs, dt, gates = _inproj(x, shift1, scale1, norm1_w[l].reshape(1, d), wq, wk, wv, wx, wz,
                                                wdt, wg, gsum, qa, qb, ka, kb, vone, dtb, ts)
        score_bound = (q_scale * HEAD_DIM * jnp.max(jnp.abs(q_norm_w[l])) * jnp.max(jnp.abs(k_norm_w[l])))
        attn = lax.cond(score_bound <= MAX_UNSHIFTED_SCORE,
                        lambda *a: _attention(*a, tq, tk, _attn_unshifted_kernel),
                        lambda *a: _attention(*a, tq, tk, _attn_kernel), q, kt, v)

        a_row = jnp.pad(-jnp.exp(A_log[l].astype(F32)).reshape(1, 2 * SSD_H), ((0, 0), (0, LANES - 2 * SSD_H)))
        dskip = jnp.repeat(ssd_D[l].astype(F32), SSD_P).reshape(1, D_INNER)
        ssd = _ssd(xbc, dt, zs, conv_w[l], conv_b[l].reshape(1, CONV_DIM), a_row, dskip,
                   ssd_norm_w[l].reshape(1, D_INNER), tri_lo, tri_up, e2_f, e2_b)

        x1, h2 = _merge(attn, ssd, gates, x, gate1, shift2, scale2, norm2_w[l].reshape(1, d),
                        w_attn_out[l].astype(BF16), w_ssd_out[l].astype(BF16), w_o[l].astype(BF16), ts)
        x = _mlp(h2, x1, gate2, w_mlp1[l].astype(BF16), w_mlp2[l].astype(BF16), ts)
    return x
```

```python
import functools
import math

import jax
import jax.numpy as jnp
import numpy as np
from jax import lax
from jax.experimental import pallas as pl
from jax.experimental.pallas import tpu as pltpu

F32 = jnp.float32
BF16 = jnp.bfloat16

D_MODEL = 1024
GRID_W = 64
N_Q_HEADS = 16
N_KV_HEADS = 4
REP = N_Q_HEADS // N_KV_HEADS
HEAD_DIM = 64
ROPE_THETA = 10000.0
D_INNER = 2048
SSD_P = 64
SSD_H = D_INNER // SSD_P
SSD_G = 4
SSD_N = 128
HEADS_PER_GROUP = SSD_H // SSD_G
D_CONV = 5
CHUNK = 128
CONV_DIM = D_INNER + 2 * SSD_G * SSD_N
D_FF = 4 * D_MODEL
EPS = 1e-6

LANES = 128
V_ROWS = 80
CONV_HALO = 16
NEG_BIG = -1e30
MAX_UNSHIFTED_SCORE = 60.0
VMEM_LIMIT = 56 * 1024 * 1024


def _dot(a, b):
    return jnp.dot(a, b, preferred_element_type=F32)


def _dot_nt(a, b):
    return lax.dot_general(a, b, (((1,), (1,)), ((), ())), preferred_element_type=F32)


def _split3(v):
    hi = v.astype(BF16)
    r = v - hi.astype(F32)
    mid = r.astype(BF16)
    lo = (r - mid.astype(F32)).astype(BF16)
    return hi, mid, lo


def _const_spec(shape):
    return pl.BlockSpec(shape, lambda *_: (0,) * len(shape))


def _ada_kernel(c_ref, w_ref, b_ref, o_ref):
    c = c_ref[...]
    sc = c * jax.nn.sigmoid(c)
    a_hi, a_mid, a_lo = _split3(sc)
    w_hi, w_mid, w_lo = _split3(w_ref[...])
    acc = _dot(a_hi, w_hi)
    acc += _dot(a_hi, w_mid) + _dot(a_mid, w_hi)
    acc += _dot(a_hi, w_lo) + _dot(a_mid, w_mid) + _dot(a_lo, w_hi)
    o_ref[...] = acc + b_ref[...]


def _ada(c, w_ada, b_ada):
    b, d = c.shape
    n = w_ada.shape[1]
    tn = 1536
    return pl.pallas_call(
        _ada_kernel,
        grid=(n // tn,),
        in_specs=[
            pl.BlockSpec((b, d), lambda j: (0, 0)),
            pl.BlockSpec((d, tn), lambda j: (0, j)),
            pl.BlockSpec((1, tn), lambda j: (0, j)),
        ],
        out_specs=pl.BlockSpec((b, tn), lambda j: (0, j)),
        out_shape=jax.ShapeDtypeStruct((b, n), F32),
        compiler_params=pltpu.CompilerParams(dimension_semantics=("arbitrary",), vmem_limit_bytes=VMEM_LIMIT),
        name="ada",
    )(c, w_ada, b_ada.reshape(1, n))


def _norm_rope_block(t, ss, ta, tb, swap_lo):
    n = t * lax.rsqrt(ss + EPS)
    fwd = pltpu.roll(n, LANES - 16, axis=1)
    bwd = pltpu.roll(n, 16, axis=1)
    sw = jnp.where(swap_lo, fwd, bwd)
    return n * ta + sw * tb


def _inproj_kernel(x_ref, shift_ref, scale_ref, nw_ref, wq_ref, wk_ref, wv_ref, wx_ref, wz_ref, wdt_ref, wg_ref,
                   gsum_ref, qa_ref, qb_ref, ka_ref, kb_ref, dtb_ref,
                   q_out, k_out, vt_out, xbc_out, z_out, dt_out, g_out):
    x = x_ref[...]
    ts = x.shape[0]
    ms = jnp.mean(x * x, axis=-1, keepdims=True)
    xn = x * lax.rsqrt(ms + EPS)
    h = (xn * (nw_ref[...] * (1.0 + scale_ref[...])) + shift_ref[...]).astype(BF16)

    lane = lax.broadcasted_iota(jnp.int32, (ts, LANES), 1)
    swap_lo = (lane & 31) < 16
    gsum = gsum_ref[...]

    qf = _dot(h, wq_ref[...])
    qa = qa_ref[...]
    qb = qb_ref[...]
    for blk in range(N_Q_HEADS * HEAD_DIM // 256):
        t = qf[:, blk * 256:(blk + 1) * 256]
        ss = _dot((t * t).astype(BF16), gsum)
        for half in range(2):
            sl = slice(half * LANES, (half + 1) * LANES)
            rt = _norm_rope_block(t[:, sl], ss[:, sl], qa, qb, swap_lo).T.astype(BF16)
            hd = blk * 4 + half * 2
            q_out[hd] = rt[:HEAD_DIM, :]
            q_out[hd + 1] = rt[HEAD_DIM:, :]

    kf = _dot(h, wk_ref[...])
    ssk = _dot((kf * kf).astype(BF16), gsum)
    ka = ka_ref[...]
    kb = kb_ref[...]
    for half in range(2):
        sl = slice(half * LANES, (half + 1) * LANES)
        r = _norm_rope_block(kf[:, sl], ssk[:, sl], ka, kb, swap_lo).astype(BF16)
        k_out[2 * half] = r[:, :HEAD_DIM]
        k_out[2 * half + 1] = r[:, HEAD_DIM:]

    vt = _dot_nt(wv_ref[...], h).astype(BF16)
    ones_rows = (lax.broadcasted_iota(jnp.int32, (V_ROWS - HEAD_DIM, ts), 0) == 0).astype(BF16)
    for g in range(N_KV_HEADS):
        vt_out[g, :HEAD_DIM, :] = vt[g * HEAD_DIM:(g + 1) * HEAD_DIM, :]
        vt_out[g, HEAD_DIM:, :] = ones_rows

    for blk in range(CONV_DIM // 512):
        sl = slice(blk * 512, (blk + 1) * 512)
        xbc_out[:, sl] = _dot(h, wx_ref[:, sl]).astype(BF16)

    for blk in range(D_INNER // 512):
        sl = slice(blk * 512, (blk + 1) * 512)
        zf = _dot(h, wz_ref[:, sl])
        z_out[:, sl] = (zf * jax.nn.sigmoid(zf)).astype(BF16)

    dtf = _dot(h, wdt_ref[...]) + dtb_ref[...]
    dt_out[...] = jnp.maximum(dtf, 0.0) + jnp.log1p(jnp.exp(-jnp.abs(dtf)))

    for blk in range(2 * D_MODEL // 512):
        sl = slice(blk * 512, (blk + 1) * 512)
        g_out[:, sl] = jax.nn.sigmoid(_dot(h, wg_ref[:, sl])).astype(BF16)


def _inproj(x, shift1, scale1, norm1_w, wq, wk, wv, wx, wz, wdt, wg, gsum, qa, qb, ka, kb, dtb, ts):
    b, s, d = x.shape
    n_t = s // ts
    row = lambda bi, i: (bi, i, 0)
    per_b = pl.BlockSpec((None, 1, d), lambda bi, i: (bi, 0, 0))
    tab = pl.BlockSpec((ts, LANES), lambda bi, i: (i, 0))
    in_specs = [
        pl.BlockSpec((None, ts, d), row), per_b, per_b, _const_spec((1, d)),
        _const_spec(wq.shape), _const_spec(wk.shape), _const_spec(wv.shape), _const_spec(wx.shape),
        _const_spec(wz.shape), _const_spec(wdt.shape), _const_spec(wg.shape), _const_spec(gsum.shape),
        tab, tab, tab, tab, _const_spec(dtb.shape),
    ]
    out_specs = [
        pl.BlockSpec((None, N_Q_HEADS, HEAD_DIM, ts), lambda bi, i: (bi, 0, 0, i)),
        pl.BlockSpec((None, N_KV_HEADS, ts, HEAD_DIM), lambda bi, i: (bi, 0, i, 0)),
        pl.BlockSpec((None, N_KV_HEADS, V_ROWS, ts), lambda bi, i: (bi, 0, 0, i)),
        pl.BlockSpec((None, ts, CONV_DIM), row),
        pl.BlockSpec((None, ts, D_INNER), row),
        pl.BlockSpec((None, ts, LANES), row),
        pl.BlockSpec((None, ts, 2 * D_MODEL), row),
    ]
    out_shape = [
        jax.ShapeDtypeStruct((b, N_Q_HEADS, HEAD_DIM, s), BF16),
        jax.ShapeDtypeStruct((b, N_KV_HEADS, s, HEAD_DIM), BF16),
        jax.ShapeDtypeStruct((b, N_KV_HEADS, V_ROWS, s), BF16),
        jax.ShapeDtypeStruct((b, s, CONV_DIM), BF16),
        jax.ShapeDtypeStruct((b, s, D_INNER), BF16),
        jax.ShapeDtypeStruct((b, s, LANES), F32),
        jax.ShapeDtypeStruct((b, s, 2 * D_MODEL), BF16),
    ]
    return pl.pallas_call(
        _inproj_kernel,
        grid=(b, n_t),
        in_specs=in_specs,
        out_specs=out_specs,
        out_shape=out_shape,
        compiler_params=pltpu.CompilerParams(dimension_semantics=("parallel", "parallel"),
                                             vmem_limit_bytes=VMEM_LIMIT),
        name="in_proj",
    )(x, shift1, scale1, norm1_w, wq, wk, wv, wx, wz, wdt, wg, gsum, qa, qb, ka, kb, dtb)


def _load_qt(qt_ref):
    return jnp.concatenate([qt_ref[r] for r in range(REP)], axis=1)


def _finish_attn(acc, o_ref, tq):
    o_t = acc[:HEAD_DIM] / acc[HEAD_DIM:HEAD_DIM + 1]
    o4 = jnp.concatenate([o_t[:, r * tq:(r + 1) * tq] for r in range(REP)], axis=0)
    o_ref[...] = o4.T.astype(o_ref.dtype)


def _attn_kernel(qt_ref, k_ref, vt_ref, o_ref, *, tq, tk):
    s_len = k_ref.shape[0]
    cols = REP * tq
    qt = _load_qt(qt_ref)

    def body(j, carry):
        m, acc = carry
        off = pl.multiple_of(j * tk, tk)
        sc = _dot(k_ref[pl.ds(off, tk), :], qt)
        m_new = jnp.maximum(m, jnp.max(sc, axis=0, keepdims=True))
        p = jnp.exp2(sc - m_new)
        alpha = jnp.exp2(m - m_new)
        acc = alpha * acc + _dot(vt_ref[:, pl.ds(off, tk)], p.astype(BF16))
        return m_new, acc

    m0 = jnp.full((1, cols), -jnp.inf, F32)
    acc0 = jnp.zeros((V_ROWS, cols), F32)
    _, acc = lax.fori_loop(0, s_len // tk, body, (m0, acc0))
    _finish_attn(acc, o_ref, tq)


def _attn_unshifted_kernel(qt_ref, k_ref, vt_ref, o_ref, *, tq, tk):
    s_len = k_ref.shape[0]
    nk = s_len // tk
    cols = REP * tq
    qt = _load_qt(qt_ref)

    def probs(j):
        off = pl.multiple_of(j * tk, tk)
        return jnp.exp2(_dot(k_ref[pl.ds(off, tk), :], qt)).astype(BF16)

    def pv(p, j):
        off = pl.multiple_of(j * tk, tk)
        return _dot(vt_ref[:, pl.ds(off, tk)], p)

    def body(j, carry):
        p, acc = carry
        p_next = probs(j + 1)
        return p_next, acc + pv(p, j)

    p, acc = lax.fori_loop(0, nk - 1, body, (probs(0), jnp.zeros((V_ROWS, cols), F32)), unroll=True)
    _finish_attn(acc + pv(p, nk - 1), o_ref, tq)


def _attention(qt, k, vt, tq, tk, body):
    b, _, _, s = qt.shape
    return pl.pallas_call(
        functools.partial(body, tq=tq, tk=tk),
        grid=(b, N_KV_HEADS, s // tq),
        in_specs=[
            pl.BlockSpec((None, REP, HEAD_DIM, tq), lambda bi, g, i: (bi, g, 0, i)),
            pl.BlockSpec((None, None, s, HEAD_DIM), lambda bi, g, i: (bi, g, 0, 0)),
            pl.BlockSpec((None, None, V_ROWS, s), lambda bi, g, i: (bi, g, 0, 0)),
        ],
        out_specs=pl.BlockSpec((None, tq, REP * HEAD_DIM), lambda bi, g, i: (bi, i, g)),
        out_shape=jax.ShapeDtypeStruct((b, s, N_Q_HEADS * HEAD_DIM), BF16),
        compiler_params=pltpu.CompilerParams(dimension_semantics=("parallel", "parallel", "arbitrary"),
                                             vmem_limit_bytes=VMEM_LIMIT),
        name="attn_online" if body is _attn_kernel else "attn_unshifted",
    )(qt, k, vt)


def _expand_heads(v, e2):
    hi = v.astype(BF16)
    mid = (v - hi.astype(F32)).astype(BF16)
    return _dot(jnp.concatenate([hi, mid], axis=1), e2)


def _ssd_scan_chunk(xs, bmat, cmat, dt, a_row, tmat, e2, st_ref, y_ref, *, slot0, reverse):
    L = xs.shape[0]
    a = dt * a_row
    a_hi, a_mid, a_lo = _split3(a)
    cum = _dot(tmat, a_hi) + _dot(tmat, a_mid) + _dot(tmat, a_lo)
    cum_t = cum.T
    dt_t = dt.T
    total = cum[0:1, :] if reverse else cum[L - 1:L, :]
    w_exp = _expand_heads(jnp.exp(total - cum) * dt, e2)
    ecum_exp = _expand_heads(jnp.exp(cum), e2)
    dec_exp = _expand_heads(jnp.broadcast_to(jnp.exp(total), (8, LANES)), e2)[0:1, :]

    li = lax.broadcasted_iota(jnp.int32, (L, L), 0)
    si = lax.broadcasted_iota(jnp.int32, (L, L), 1)
    keep = (si >= li) if reverse else (li >= si)
    lane = lax.broadcasted_iota(jnp.int32, (L, LANES), 1)
    first_head = lane < SSD_P

    for g in range(SSD_G):
        gs = slice(g * SSD_N, (g + 1) * SSD_N)
        cols = slice(g * HEADS_PER_GROUP * SSD_P, (g + 1) * HEADS_PER_GROUP * SSD_P)
        bg = bmat[:, gs]
        cg_bf = cmat[:, gs].astype(BF16)
        cb = _dot_nt(cg_bf, bg.astype(BF16))
        st_g = st_ref[:, cols]
        y_off = _dot(cg_bf, st_g.astype(BF16)) * ecum_exp[:, cols]
        xw = (xs[:, cols] * w_exp[:, cols]).astype(BF16)
        st_ref[:, cols] = dec_exp[:, cols] * st_g + _dot(bg.T.astype(BF16), xw)
        for pr in range(HEADS_PER_GROUP // 2):
            h0 = g * HEADS_PER_GROUP + 2 * pr
            ms = []
            for hh in (h0, h0 + 1):
                j = slot0 + hh
                arg = cum[:, j:j + 1] - cum_t[j:j + 1, :]
                dm = jnp.exp(jnp.where(keep, arg, NEG_BIG)) * dt_t[j:j + 1, :]
                ms.append((cb * dm).astype(BF16))
            lhs = jnp.concatenate(ms, axis=1)
            pc = slice(h0 * SSD_P, h0 * SSD_P + LANES)
            xp = xs[:, pc]
            rhs = jnp.concatenate([jnp.where(first_head, xp, 0.0).astype(BF16),
                                   jnp.where(first_head, 0.0, xp).astype(BF16)], axis=0)
            y_ref[:, pc] = _dot(lhs, rhs) + y_off[:, pr * LANES:(pr + 1) * LANES]


def _ssd_fwd_kernel(x_ref, xprev_ref, xnext_ref, dt_ref, cw_ref, cb_ref, a_ref, dskip_ref, tmat_ref, e2_ref,
                    xc_out, y_out, ext_ref, st_ref, y_ref):
    c = pl.program_id(1)
    nc = pl.num_programs(1)
    L = x_ref.shape[0]

    @pl.when(c == 0)
    def _():
        st_ref[...] = jnp.zeros_like(st_ref)

    prev = xprev_ref[...].astype(F32)[CONV_HALO - 8:, :]
    nxt = xnext_ref[...].astype(F32)[:8, :]
    ext_ref[0:8, :] = jnp.where(c > 0, prev, 0.0)
    ext_ref[8:8 + L, :] = x_ref[...].astype(F32)
    ext_ref[8 + L:16 + L, :] = jnp.where(c < nc - 1, nxt, 0.0)

    pad = (D_CONV - 1) // 2
    acts = []
    for blk in range(CONV_DIM // 512):
        sl = slice(blk * 512, (blk + 1) * 512)
        acc = cb_ref[:, sl] + cw_ref[0:1, sl] * ext_ref[8 - pad:8 - pad + L, sl]
        for k in range(1, D_CONV):
            acc = acc + cw_ref[k:k + 1, sl] * ext_ref[8 - pad + k:8 - pad + k + L, sl]
        act = acc * jax.nn.sigmoid(acc)
        xc_out[:, sl] = act.astype(BF16)
        acts.append(act)
    xs = jnp.concatenate(acts[:D_INNER // 512], axis=1)
    bmat = acts[D_INNER // 512]
    cmat = acts[D_INNER // 512 + 1]

    _ssd_scan_chunk(xs, bmat, cmat, dt_ref[...], a_ref[...], tmat_ref[...], e2_ref[...], st_ref, y_ref,
                    slot0=0, reverse=False)
    y_out[...] = y_ref[...] + dskip_ref[...] * xs


def _ssd_bwd_kernel(xc_ref, dt_ref, yf_ref, zs_ref, a_ref, nw_ref, tmat_ref, e2_ref, o_ref, st_ref, y_ref):
    c = pl.program_id(1)

    @pl.when(c == 0)
    def _():
        st_ref[...] = jnp.zeros_like(st_ref)

    xc = xc_ref[...].astype(F32)
    xs = xc[:, :D_INNER]
    bmat = xc[:, D_INNER:D_INNER + SSD_G * SSD_N]
    cmat = xc[:, D_INNER + SSD_G * SSD_N:]
    _ssd_scan_chunk(xs, bmat, cmat, dt_ref[...], a_ref[...], tmat_ref[...], e2_ref[...], st_ref, y_ref,
                    slot0=SSD_H, reverse=True)
    y = y_ref[...] + yf_ref[...]
    gated = y * zs_ref[...].astype(F32)
    ms = jnp.mean(gated * gated, axis=-1, keepdims=True)
    o_ref[...] = (gated * lax.rsqrt(ms + EPS) * nw_ref[...]).astype(o_ref.dtype)


def _ssd(xbc, dt, zs, conv_w, conv_b, a_row, dskip, ssd_norm_w, tri_lo, tri_up, e2_f, e2_b):
    b, s, _ = xbc.shape
    nc = s // CHUNK
    hb = CHUNK // CONV_HALO
    n_halo = s // CONV_HALO
    row = lambda bi, c: (bi, c, 0)
    params = pltpu.CompilerParams(dimension_semantics=("parallel", "arbitrary"), vmem_limit_bytes=VMEM_LIMIT)
    xc, yf = pl.pallas_call(
        _ssd_fwd_kernel,
        grid=(b, nc),
        in_specs=[
            pl.BlockSpec((None, CHUNK, CONV_DIM), row),
            pl.BlockSpec((None, CONV_HALO, CONV_DIM), lambda bi, c: (bi, jnp.maximum(c * hb - 1, 0), 0)),
            pl.BlockSpec((None, CONV_HALO, CONV_DIM), lambda bi, c: (bi, jnp.minimum((c + 1) * hb, n_halo - 1), 0)),
            pl.BlockSpec((None, CHUNK, LANES), row),
            _const_spec(conv_w.shape), _const_spec(conv_b.shape), _const_spec(a_row.shape),
            _const_spec(dskip.shape), _const_spec(tri_lo.shape), _const_spec(e2_f.shape),
        ],
        out_specs=[pl.BlockSpec((None, CHUNK, CONV_DIM), row), pl.BlockSpec((None, CHUNK, D_INNER), row)],
        out_shape=[jax.ShapeDtypeStruct((b, s, CONV_DIM), BF16), jax.ShapeDtypeStruct((b, s, D_INNER), F32)],
        scratch_shapes=[pltpu.VMEM((CHUNK + 16, CONV_DIM), F32), pltpu.VMEM((SSD_N, D_INNER), F32),
                        pltpu.VMEM((CHUNK, D_INNER), F32)],
        compiler_params=params,
        name="ssd_fwd",
    )(xbc, xbc, xbc, dt, conv_w, conv_b, a_row, dskip, tri_lo, e2_f)

    rev = lambda bi, c: (bi, nc - 1 - c, 0)
    return pl.pallas_call(
        _ssd_bwd_kernel,
        grid=(b, nc),
        in_specs=[
            pl.BlockSpec((None, CHUNK, CONV_DIM), rev),
            pl.BlockSpec((None, CHUNK, LANES), rev),
            pl.BlockSpec((None, CHUNK, D_INNER), rev),
            pl.BlockSpec((None, CHUNK, D_INNER), rev),
            _const_spec(a_row.shape), _const_spec(ssd_norm_w.shape), _const_spec(tri_up.shape),
            _const_spec(e2_b.shape),
        ],
        out_specs=pl.BlockSpec((None, CHUNK, D_INNER), rev),
        out_shape=jax.ShapeDtypeStruct((b, s, D_INNER), BF16),
        scratch_shapes=[pltpu.VMEM((SSD_N, D_INNER), F32), pltpu.VMEM((CHUNK, D_INNER), F32)],
        compiler_params=params,
        name="ssd_bwd",
    )(xc, dt, yf, zs, a_row, ssd_norm_w, tri_up, e2_b)


def _merge_kernel(attn_ref, ssd_ref, g_ref, x_ref, gate1_ref, shift2_ref, scale2_ref, nw2_ref,
                  wao_ref, wso_ref, wo_ref, x1_out, h2_out):
    a = _dot(attn_ref[...], wao_ref[...])
    sm = _dot(ssd_ref[...], wso_ref[...])
    g = g_ref[...].astype(F32)
    merged = (g[:, :D_MODEL] * a + g[:, D_MODEL:] * sm).astype(BF16)
    x1 = x_ref[...] + gate1_ref[...] * _dot(merged, wo_ref[...])
    x1_out[...] = x1
    ms = jnp.mean(x1 * x1, axis=-1, keepdims=True)
    xn = x1 * lax.rsqrt(ms + EPS)
    h2_out[...] = (xn * (nw2_ref[...] * (1.0 + scale2_ref[...])) + shift2_ref[...]).astype(BF16)


def _merge(attn, ssd, gates, x, gate1, shift2, scale2, norm2_w, wao, wso, wo, tm):
    b, s, d = x.shape
    row = lambda bi, i: (bi, i, 0)
    per_b = pl.BlockSpec((None, 1, d), lambda bi, i: (bi, 0, 0))
    return pl.pallas_call(
        _merge_kernel,
        grid=(b, s // tm),
        in_specs=[
            pl.BlockSpec((None, tm, attn.shape[-1]), row), pl.BlockSpec((None, tm, D_INNER), row),
            pl.BlockSpec((None, tm, 2 * d), row), pl.BlockSpec((None, tm, d), row),
            per_b, per_b, per_b, _const_spec((1, d)),
            _const_spec(wao.shape), _const_spec(wso.shape), _const_spec(wo.shape),
        ],
        out_specs=[pl.BlockSpec((None, tm, d), row), pl.BlockSpec((None, tm, d), row)],
        out_shape=[jax.ShapeDtypeStruct((b, s, d), F32), jax.ShapeDtypeStruct((b, s, d), BF16)],
        compiler_params=pltpu.CompilerParams(dimension_semantics=("parallel", "parallel"),
                                             vmem_limit_bytes=VMEM_LIMIT),
        name="merge",
    )(attn, ssd, gates, x, gate1, shift2, scale2, norm2_w, wao, wso, wo)


def _mlp_kernel(h2_ref, x1_ref, gate2_ref, w1_ref, w2_ref, o_ref):
    h2 = h2_ref[...]
    acc = jnp.zeros(x1_ref.shape, F32)
    for blk in range(D_FF // 1024):
        sl = slice(blk * 1024, (blk + 1) * 1024)
        u = jnp.maximum(_dot(h2, w1_ref[:, sl]), 0.0)
        acc = acc + _dot((u * u).astype(BF16), w2_ref[sl, :])
    o_ref[...] = x1_ref[...] + gate2_ref[...] * acc


def _mlp(h2, x1, gate2, w1, w2, tm):
    b, s, d = x1.shape
    row = lambda bi, i: (bi, i, 0)
    return pl.pallas_call(
        _mlp_kernel,
        grid=(b, s // tm),
        in_specs=[
            pl.BlockSpec((None, tm, d), row), pl.BlockSpec((None, tm, d), row),
            pl.BlockSpec((None, 1, d), lambda bi, i: (bi, 0, 0)),
            _const_spec(w1.shape), _const_spec(w2.shape),
        ],
        out_specs=pl.BlockSpec((None, tm, d), row),
        out_shape=jax.ShapeDtypeStruct((b, s, d), F32),
        compiler_params=pltpu.CompilerParams(dimension_semantics=("parallel", "parallel"),
                                             vmem_limit_bytes=VMEM_LIMIT),
        name="mlp",
    )(h2, x1, gate2, w1, w2)


def _swap_halves(w):
    return w.reshape(2, 2, 16)[:, ::-1, :].reshape(HEAD_DIM)


def _rope_tables(s, norm_w, scale):
    pos = jnp.arange(s, dtype=jnp.int32)
    pos_row = (pos // GRID_W).astype(F32)
    pos_col = (pos % GRID_W).astype(F32)
    axis_dim = HEAD_DIM // 2
    inv_freq = ROPE_THETA ** (-jnp.arange(0, axis_dim, 2, dtype=F32) / axis_dim)
    ang_r = pos_row[:, None] * inv_freq[None, :]
    ang_c = pos_col[:, None] * inv_freq[None, :]
    cos = jnp.concatenate([jnp.cos(ang_r)] * 2 + [jnp.cos(ang_c)] * 2, axis=-1)
    sin_r, sin_c = jnp.sin(ang_r), jnp.sin(ang_c)
    sin = jnp.concatenate([-sin_r, sin_r, -sin_c, sin_c], axis=-1)
    ta = cos * norm_w[None, :] * scale
    tb = sin * _swap_halves(norm_w)[None, :] * scale
    return jnp.tile(ta, (1, 2)), jnp.tile(tb, (1, 2))


def _head_expansion(slot0):
    e = np.zeros((LANES, D_INNER), np.float32)
    for h in range(SSD_H):
        e[slot0 + h, h * SSD_P:(h + 1) * SSD_P] = 1.0
    return jnp.asarray(np.concatenate([e, e], axis=0), BF16)


def _pick_tile(s, pref):
    t = min(pref, s)
    assert s % t == 0
    return t


def kernel(x, c, w_ada, b_ada, norm1_w, norm2_w, w_in, q_norm_w, k_norm_w, conv_w, conv_b, A_log, dt_bias,
           ssd_D, ssd_norm_w, w_attn_out, w_ssd_out, w_o, w_mlp1, w_mlp2):
    b, s, d = x.shape
    depth = w_ada.shape[0]
    assert d == D_MODEL and s % CHUNK == 0 and s % GRID_W == 0

    qdim = N_Q_HEADS * HEAD_DIM
    kvdim = N_KV_HEADS * HEAD_DIM
    o_k, o_v, o_x = qdim, qdim + kvdim, qdim + 2 * kvdim
    o_z = o_x + CONV_DIM
    o_dt = o_z + D_INNER
    o_g = o_dt + 2 * SSD_H

    gsum = jnp.asarray(np.kron(np.eye(4, dtype=np.float32), np.full((HEAD_DIM, HEAD_DIM), 1.0 / HEAD_DIM,
                                                                       np.float32)), BF16)
    tri = np.tril(np.ones((CHUNK, CHUNK), np.float32))
    tri_lo = jnp.asarray(tri, BF16)
    tri_up = jnp.asarray(tri.T, BF16)
    e2_f = _head_expansion(0)
    e2_b = _head_expansion(SSD_H)
    ts = _pick_tile(s, 512)
    tq = _pick_tile(s, 128)
    tk = _pick_tile(s, 512)

    for l in range(depth):
        mod = _ada(c, w_ada[l], b_ada[l])
        shift1, scale1, gate1, shift2, scale2, gate2 = [m[:, None, :] for m in jnp.split(mod, 6, axis=-1)]

        w = w_in[l]
        wq = w[:, :o_k].astype(BF16)
        wk = w[:, o_k:o_v].astype(BF16)
        wv = w[:, o_v:o_x].T.astype(BF16)
        wx = w[:, o_x:o_z].astype(BF16)
        wz = w[:, o_z:o_dt].astype(BF16)
        wdt = jnp.pad(w[:, o_dt:o_g], ((0, 0), (0, LANES - 2 * SSD_H))).astype(BF16)
        wg = w[:, o_g:].astype(BF16)
        dtb = jnp.pad(dt_bias[l].reshape(1, 2 * SSD_H), ((0, 0), (0, LANES - 2 * SSD_H)))
        q_scale = math.log2(math.e) * HEAD_DIM ** -0.5
        qa, qb = _rope_tables(s, q_norm_w[l], q_scale)
        ka, kb = _rope_tables(s, k_norm_w[l], 1.0)

        qt, k, vt, xbc, zs, dt, gates = _inproj(x, shift1, scale1, norm1_w[l].reshape(1, d), wq, wk, wv, wx, wz,
                                                wdt, wg, gsum, qa, qb, ka, kb, dtb, ts)
        score_bound = (q_scale * HEAD_DIM * jnp.max(jnp.abs(q_norm_w[l])) * jnp.max(jnp.abs(k_norm_w[l])))
        attn = lax.cond(score_bound <= MAX_UNSHIFTED_SCORE,
                        lambda *a: _attention(*a, tq, tk, _attn_unshifted_kernel),
                        lambda *a: _attention(*a, tq, tk, _attn_kernel), qt, k, vt)

        a_row = jnp.pad(-jnp.exp(A_log[l].astype(F32)).reshape(1, 2 * SSD_H), ((0, 0), (0, LANES - 2 * SSD_H)))
        dskip = jnp.repeat(ssd_D[l].astype(F32), SSD_P).reshape(1, D_INNER)
        ssd = _ssd(xbc, dt, zs, conv_w[l], conv_b[l].reshape(1, CONV_DIM), a_row, dskip,
                   ssd_norm_w[l].reshape(1, D_INNER), tri_lo, tri_up, e2_f, e2_b)

        x1, h2 = _merge(attn, ssd, gates, x, gate1, shift2, scale2, norm2_w[l].reshape(1, d),
                        w_attn_out[l].astype(BF16), w_ssd_out[l].astype(BF16), w_o[l].astype(BF16), ts)
        x = _mlp(h2, x1, gate2, w_mlp1[l].astype(BF16), w_mlp2[l].astype(BF16), ts)
    return x
```

```python
import functools
import math

import jax
import jax.numpy as jnp
import numpy as np
from jax import lax
from jax.experimental import pallas as pl
from jax.experimental.pallas import tpu as pltpu

F32 = jnp.float32
BF16 = jnp.bfloat16

D_MODEL = 1024
GRID_W = 64
N_Q_HEADS = 16
N_KV_HEADS = 4
REP = N_Q_HEADS // N_KV_HEADS
HEAD_DIM = 64
ROPE_THETA = 10000.0
D_INNER = 2048
SSD_P = 64
SSD_H = D_INNER // SSD_P
SSD_G = 4
SSD_N = 128
HEADS_PER_GROUP = SSD_H // SSD_G
D_CONV = 5
CHUNK = 128
CONV_DIM = D_INNER + 2 * SSD_G * SSD_N
D_FF = 4 * D_MODEL
EPS = 1e-6

LANES = 128
V_ROWS = 80
CONV_HALO = 16
CHUNKS_PER_STEP = 2
SHIFT_K = 256
NEG_BIG = -1e30
MAX_UNSHIFTED_SCORE = 60.0
VMEM_LIMIT = 56 * 1024 * 1024


def _dot(a, b):
    return jnp.dot(a, b, preferred_element_type=F32)


def _dot_nt(a, b):
    return lax.dot_general(a, b, (((1,), (1,)), ((), ())), preferred_element_type=F32)


def _split3(v):
    hi = v.astype(BF16)
    r = v - hi.astype(F32)
    mid = r.astype(BF16)
    lo = (r - mid.astype(F32)).astype(BF16)
    return hi, mid, lo


def _const_spec(shape):
    return pl.BlockSpec(shape, lambda *_: (0,) * len(shape))


def _ada_kernel(c_ref, w_ref, b_ref, o_ref):
    c = c_ref[...]
    sc = c * jax.nn.sigmoid(c)
    a_hi, a_mid, a_lo = _split3(sc)
    w_hi, w_mid, w_lo = _split3(w_ref[...])
    acc = _dot(a_hi, w_hi)
    acc += _dot(a_hi, w_mid) + _dot(a_mid, w_hi)
    acc += _dot(a_hi, w_lo) + _dot(a_mid, w_mid) + _dot(a_lo, w_hi)
    o_ref[...] = acc + b_ref[...]


def _ada(c, w_ada, b_ada):
    b, d = c.shape
    n = w_ada.shape[1]
    tn = 1536
    return pl.pallas_call(
        _ada_kernel,
        grid=(n // tn,),
        in_specs=[
            pl.BlockSpec((b, d), lambda j: (0, 0)),
            pl.BlockSpec((d, tn), lambda j: (0, j)),
            pl.BlockSpec((1, tn), lambda j: (0, j)),
        ],
        out_specs=pl.BlockSpec((b, tn), lambda j: (0, j)),
        out_shape=jax.ShapeDtypeStruct((b, n), F32),
        compiler_params=pltpu.CompilerParams(dimension_semantics=("arbitrary",), vmem_limit_bytes=VMEM_LIMIT),
        name="ada",
    )(c, w_ada, b_ada.reshape(1, n))


def _norm_rope_block(t, ss, ta, tb, swap_lo):
    n = t * lax.rsqrt(ss + EPS)
    fwd = pltpu.roll(n, LANES - 16, axis=1)
    bwd = pltpu.roll(n, 16, axis=1)
    sw = jnp.where(swap_lo, fwd, bwd)
    return n * ta + sw * tb


def _inproj_kernel(x_ref, shift_ref, scale_ref, nw_ref, wq_ref, wk_ref, wv_ref, wx_ref, wz_ref, wdt_ref, wg_ref,
                   gsum_ref, qa_ref, qb_ref, ka_ref, kb_ref, dtb_ref,
                   q_out, k_out, vt_out, xbc_out, z_out, dt_out, g_out):
    x = x_ref[...]
    ts = x.shape[0]
    ms = jnp.mean(x * x, axis=-1, keepdims=True)
    xn = x * lax.rsqrt(ms + EPS)
    h = (xn * (nw_ref[...] * (1.0 + scale_ref[...])) + shift_ref[...]).astype(BF16)

    lane = lax.broadcasted_iota(jnp.int32, (ts, LANES), 1)
    swap_lo = (lane & 31) < 16
    gsum = gsum_ref[...]

    qf = _dot(h, wq_ref[...])
    qa = qa_ref[...]
    qb = qb_ref[...]
    for blk in range(N_Q_HEADS * HEAD_DIM // 256):
        t = qf[:, blk * 256:(blk + 1) * 256]
        ss = _dot((t * t).astype(BF16), gsum)
        for half in range(2):
            sl = slice(half * LANES, (half + 1) * LANES)
            rt = _norm_rope_block(t[:, sl], ss[:, sl], qa, qb, swap_lo).T.astype(BF16)
            hd = blk * 4 + half * 2
            q_out[hd] = rt[:HEAD_DIM, :]
            q_out[hd + 1] = rt[HEAD_DIM:, :]

    kf = _dot(h, wk_ref[...])
    ssk = _dot((kf * kf).astype(BF16), gsum)
    ka = ka_ref[...]
    kb = kb_ref[...]
    for half in range(2):
        sl = slice(half * LANES, (half + 1) * LANES)
        r = _norm_rope_block(kf[:, sl], ssk[:, sl], ka, kb, swap_lo).astype(BF16)
        k_out[2 * half] = r[:, :HEAD_DIM]
        k_out[2 * half + 1] = r[:, HEAD_DIM:]

    vt = _dot_nt(wv_ref[...], h).astype(BF16)
    ones_rows = (lax.broadcasted_iota(jnp.int32, (V_ROWS - HEAD_DIM, ts), 0) == 0).astype(BF16)
    for g in range(N_KV_HEADS):
        vt_out[g, :HEAD_DIM, :] = vt[g * HEAD_DIM:(g + 1) * HEAD_DIM, :]
        vt_out[g, HEAD_DIM:, :] = ones_rows

    for blk in range(CONV_DIM // 512):
        sl = slice(blk * 512, (blk + 1) * 512)
        xbc_out[:, sl] = _dot(h, wx_ref[:, sl]).astype(BF16)

    for blk in range(D_INNER // 512):
        sl = slice(blk * 512, (blk + 1) * 512)
        zf = _dot(h, wz_ref[:, sl])
        z_out[:, sl] = (zf * jax.nn.sigmoid(zf)).astype(BF16)

    dtf = _dot(h, wdt_ref[...]) + dtb_ref[...]
    dt_out[...] = jnp.maximum(dtf, 0.0) + jnp.log1p(jnp.exp(-jnp.abs(dtf)))

    for blk in range(2 * D_MODEL // 512):
        sl = slice(blk * 512, (blk + 1) * 512)
        g_out[:, sl] = jax.nn.sigmoid(_dot(h, wg_ref[:, sl])).astype(BF16)


def _inproj(x, shift1, scale1, norm1_w, wq, wk, wv, wx, wz, wdt, wg, gsum, qa, qb, ka, kb, dtb, ts):
    b, s, d = x.shape
    n_t = s // ts
    row = lambda bi, i: (bi, i, 0)
    per_b = pl.BlockSpec((None, 1, d), lambda bi, i: (bi, 0, 0))
    tab = pl.BlockSpec((ts, LANES), lambda bi, i: (i, 0))
    in_specs = [
        pl.BlockSpec((None, ts, d), row), per_b, per_b, _const_spec((1, d)),
        _const_spec(wq.shape), _const_spec(wk.shape), _const_spec(wv.shape), _const_spec(wx.shape),
        _const_spec(wz.shape), _const_spec(wdt.shape), _const_spec(wg.shape), _const_spec(gsum.shape),
        tab, tab, tab, tab, _const_spec(dtb.shape),
    ]
    out_specs = [
        pl.BlockSpec((None, N_Q_HEADS, HEAD_DIM, ts), lambda bi, i: (bi, 0, 0, i)),
        pl.BlockSpec((None, N_KV_HEADS, ts, HEAD_DIM), lambda bi, i: (bi, 0, i, 0)),
        pl.BlockSpec((None, N_KV_HEADS, V_ROWS, ts), lambda bi, i: (bi, 0, 0, i)),
        pl.BlockSpec((None, ts, CONV_DIM), row),
        pl.BlockSpec((None, ts, D_INNER), row),
        pl.BlockSpec((None, ts, LANES), row),
        pl.BlockSpec((None, ts, 2 * D_MODEL), row),
    ]
    out_shape = [
        jax.ShapeDtypeStruct((b, N_Q_HEADS, HEAD_DIM, s), BF16),
        jax.ShapeDtypeStruct((b, N_KV_HEADS, s, HEAD_DIM), BF16),
        jax.ShapeDtypeStruct((b, N_KV_HEADS, V_ROWS, s), BF16),
        jax.ShapeDtypeStruct((b, s, CONV_DIM), BF16),
        jax.ShapeDtypeStruct((b, s, D_INNER), BF16),
        jax.ShapeDtypeStruct((b, s, LANES), F32),
        jax.ShapeDtypeStruct((b, s, 2 * D_MODEL), BF16),
    ]
    return pl.pallas_call(
        _inproj_kernel,
        grid=(b, n_t),
        in_specs=in_specs,
        out_specs=out_specs,
        out_shape=out_shape,
        compiler_params=pltpu.CompilerParams(dimension_semantics=("parallel", "parallel"),
                                             vmem_limit_bytes=VMEM_LIMIT),
        name="in_proj",
    )(x, shift1, scale1, norm1_w, wq, wk, wv, wx, wz, wdt, wg, gsum, qa, qb, ka, kb, dtb)


def _load_qt(qt_ref):
    return jnp.concatenate([qt_ref[r] for r in range(REP)], axis=1)


def _finish_attn(acc, o_ref, tq):
    o_t = acc[:HEAD_DIM] / acc[HEAD_DIM:HEAD_DIM + 1]
    o4 = jnp.concatenate([o_t[:, r * tq:(r + 1) * tq] for r in range(REP)], axis=0)
    o_ref[...] = o4.T.astype(o_ref.dtype)


def _attn_kernel(qt_ref, k_ref, vt_ref, o_ref, *, tq, tk):
    s_len = k_ref.shape[0]
    cols = REP * tq
    qt = _load_qt(qt_ref)

    def body(j, carry):
        m, acc = carry
        off = pl.multiple_of(j * tk, tk)
        sc = _dot(k_ref[pl.ds(off, tk), :], qt)
        m_new = jnp.maximum(m, jnp.max(sc, axis=0, keepdims=True))
        p = jnp.exp2(sc - m_new)
        alpha = jnp.exp2(m - m_new)
        acc = alpha * acc + _dot(vt_ref[:, pl.ds(off, tk)], p.astype(BF16))
        return m_new, acc

    m0 = jnp.full((1, cols), -jnp.inf, F32)
    acc0 = jnp.zeros((V_ROWS, cols), F32)
    _, acc = lax.fori_loop(0, s_len // tk, body, (m0, acc0))
    _finish_attn(acc, o_ref, tq)


def _attn_unshifted_kernel(qt_ref, k_ref, vt_ref, o_ref, *, tq, tk):
    s_len = k_ref.shape[0]
    nk = s_len // tk
    cols = REP * tq
    qt = _load_qt(qt_ref)

    def probs(j):
        off = pl.multiple_of(j * tk, tk)
        return jnp.exp2(_dot(k_ref[pl.ds(off, tk), :], qt)).astype(BF16)

    def pv(p, j):
        off = pl.multiple_of(j * tk, tk)
        return _dot(vt_ref[:, pl.ds(off, tk)], p)

    def body(j, carry):
        p, acc = carry
        p_next = probs(j + 1)
        return p_next, acc + pv(p, j)

    p, acc = lax.fori_loop(0, nk - 1, body, (probs(0), jnp.zeros((V_ROWS, cols), F32)), unroll=True)
    _finish_attn(acc + pv(p, nk - 1), o_ref, tq)


def _attention(qt, k, vt, tq, tk, body):
    b, _, _, s = qt.shape
    return pl.pallas_call(
        functools.partial(body, tq=tq, tk=tk),
        grid=(b, N_KV_HEADS, s // tq),
        in_specs=[
            pl.BlockSpec((None, REP, HEAD_DIM, tq), lambda bi, g, i: (bi, g, 0, i)),
            pl.BlockSpec((None, None, s, HEAD_DIM), lambda bi, g, i: (bi, g, 0, 0)),
            pl.BlockSpec((None, None, V_ROWS, s), lambda bi, g, i: (bi, g, 0, 0)),
        ],
        out_specs=pl.BlockSpec((None, tq, REP * HEAD_DIM), lambda bi, g, i: (bi, i, g)),
        out_shape=jax.ShapeDtypeStruct((b, s, N_Q_HEADS * HEAD_DIM), BF16),
        compiler_params=pltpu.CompilerParams(dimension_semantics=("parallel", "parallel", "arbitrary"),
                                             vmem_limit_bytes=VMEM_LIMIT),
        name="attn_online" if body is _attn_kernel else "attn_unshifted",
    )(qt, k, vt)


def _hi_mid(v):
    hi = v.astype(BF16)
    mid = (v - hi.astype(F32)).astype(BF16)
    return jnp.concatenate([hi, mid], axis=1)


def _ssd_scan_chunk(xs, bmat, cmat, dt, a_row, tmat, e2_ref, st_ref, emit, *, slot0, reverse):
    L = xs.shape[0]
    a = dt * a_row
    a_hi, a_mid, a_lo = _split3(a)
    cum = _dot(tmat, a_hi) + _dot(tmat, a_mid) + _dot(tmat, a_lo)
    total = cum[0:1, :] if reverse else cum[L - 1:L, :]
    src_t = (cum - jnp.log(dt)).T
    w_lhs = _hi_mid(jnp.exp(total - cum) * dt)
    ecum_lhs = _hi_mid(jnp.exp(cum))
    dec_lhs = _hi_mid(jnp.broadcast_to(jnp.exp(total), (8, LANES)))

    li = lax.broadcasted_iota(jnp.int32, (L, L), 0)
    si = lax.broadcasted_iota(jnp.int32, (L, L), 1)
    keep = (si >= li) if reverse else (li >= si)
    lane = lax.broadcasted_iota(jnp.int32, (L, LANES), 1)
    first_head = lane < SSD_P

    for g in range(SSD_G):
        gs = slice(g * SSD_N, (g + 1) * SSD_N)
        cols = slice(g * HEADS_PER_GROUP * SSD_P, (g + 1) * HEADS_PER_GROUP * SSD_P)
        e2_g = e2_ref[:, cols]
        bg = bmat[:, gs]
        cg_bf = cmat[:, gs].astype(BF16)
        cb = _dot_nt(cg_bf, bg.astype(BF16))
        st_g = st_ref[:, cols]
        y_off = _dot(cg_bf, st_g.astype(BF16)) * _dot(ecum_lhs, e2_g)
        xw = (xs[:, cols] * _dot(w_lhs, e2_g)).astype(BF16)
        st_ref[:, cols] = _dot(dec_lhs, e2_g)[0:1, :] * st_g + _dot(bg.T.astype(BF16), xw)
        for pr in range(HEADS_PER_GROUP // 2):
            h0 = g * HEADS_PER_GROUP + 2 * pr
            ms = []
            for hh in (h0, h0 + 1):
                j = slot0 + hh
                arg = cum[:, j:j + 1] - src_t[j:j + 1, :]
                ms.append((cb * jnp.exp(jnp.where(keep, arg, NEG_BIG))).astype(BF16))
            lhs = jnp.concatenate(ms, axis=1)
            pc = slice(h0 * SSD_P, h0 * SSD_P + LANES)
            xp = xs[:, pc]
            rhs = jnp.concatenate([jnp.where(first_head, xp, 0.0).astype(BF16),
                                   jnp.where(first_head, 0.0, xp).astype(BF16)], axis=0)
            emit(pc, _dot(lhs, rhs) + y_off[:, pr * LANES:(pr + 1) * LANES])


def _ssd_fwd_kernel(x_ref, xprev_ref, xnext_ref, dt_ref, shift_ref, cw_ref, cb_ref, a_ref, dskip_ref, tmat_ref,
                    e2_ref, xc_out, y_out, st_ref):
    c = pl.program_id(1)
    nc = pl.num_programs(1)
    L = CHUNK

    @pl.when(c == 0)
    def _():
        st_ref[...] = jnp.zeros_like(st_ref)

    zero_halo = jnp.zeros((CONV_HALO, CONV_DIM), BF16)
    before = jnp.where(c > 0, xprev_ref[...], zero_halo)
    after = jnp.where(c < nc - 1, xnext_ref[...], zero_halo)
    shift = shift_ref[...]
    mid_tap = (D_CONV - 1) // 2
    taps = [k for k in range(D_CONV) if k != mid_tap]
    a_row = a_ref[...]
    tmat = tmat_ref[...]

    for sub in range(CHUNKS_PER_STEP):
        r0 = sub * L
        rows = slice(r0, r0 + L)
        x_mid = x_ref[rows, :]
        prev = before if sub == 0 else x_ref[r0 - CONV_HALO:r0, :]
        nxt = after if sub == CHUNKS_PER_STEP - 1 else x_ref[r0 + L:r0 + L + CONV_HALO, :]
        ext = jnp.concatenate([prev, x_mid, nxt, jnp.zeros((SHIFT_K - L - 2 * CONV_HALO, CONV_DIM), BF16)], axis=0)
        acts = []
        for blk in range(CONV_DIM // 512):
            sl = slice(blk * 512, (blk + 1) * 512)
            shifted = _dot(shift, ext[:, sl])
            acc = cb_ref[:, sl] + cw_ref[mid_tap:mid_tap + 1, sl] * x_mid[:, sl].astype(F32)
            for i, k in enumerate(taps):
                acc = acc + cw_ref[k:k + 1, sl] * shifted[i * L:(i + 1) * L, :]
            act = acc * jax.nn.sigmoid(acc)
            xc_out[rows, sl] = act.astype(BF16)
            acts.append(act)
        xs = jnp.concatenate(acts[:D_INNER // 512], axis=1)
        bmat = acts[D_INNER // 512]
        cmat = acts[D_INNER // 512 + 1]

        def emit(pc, y, rows=rows, xs=xs):
            y_out[rows, pc] = y + dskip_ref[:, pc] * xs[:, pc]

        _ssd_scan_chunk(xs, bmat, cmat, dt_ref[rows, :], a_row, tmat, e2_ref, st_ref, emit, slot0=0, reverse=False)


def _ssd_bwd_kernel(xc_ref, dt_ref, yf_ref, zs_ref, a_ref, nw_ref, tmat_ref, e2_ref, o_ref, st_ref, y_ref):
    c = pl.program_id(1)
    L = CHUNK

    @pl.when(c == 0)
    def _():
        st_ref[...] = jnp.zeros_like(st_ref)

    a_row = a_ref[...]
    tmat = tmat_ref[...]
    for sub in reversed(range(CHUNKS_PER_STEP)):
        rows = slice(sub * L, (sub + 1) * L)
        xc = xc_ref[rows, :].astype(F32)
        xs = xc[:, :D_INNER]
        bmat = xc[:, D_INNER:D_INNER + SSD_G * SSD_N]
        cmat = xc[:, D_INNER + SSD_G * SSD_N:]
        sq = []

        def emit(pc, y, rows=rows, sq=sq):
            gated = (y + yf_ref[rows, pc]) * zs_ref[rows, pc].astype(F32)
            y_ref[rows, pc] = gated
            sq.append(gated * gated)

        _ssd_scan_chunk(xs, bmat, cmat, dt_ref[rows, :], a_row, tmat, e2_ref, st_ref, emit, slot0=SSD_H,
                        reverse=True)
        ms = jnp.sum(functools.reduce(lambda u, v: u + v, sq), axis=-1, keepdims=True) * (1.0 / D_INNER)
        o_ref[rows, :] = (y_ref[rows, :] * lax.rsqrt(ms + EPS) * nw_ref[...]).astype(o_ref.dtype)


def _conv_shift_matrix():
    pad = (D_CONV - 1) // 2
    taps = [k for k in range(D_CONV) if k != pad]
    m = np.zeros((len(taps) * CHUNK, SHIFT_K), np.float32)
    for i, k in enumerate(taps):
        for l in range(CHUNK):
            m[i * CHUNK + l, CONV_HALO + l + k - pad] = 1.0
    return jnp.asarray(m, BF16)


def _ssd(xbc, dt, zs, conv_w, conv_b, a_row, dskip, ssd_norm_w, tri_lo, tri_up, e2_f, e2_b):
    b, s, _ = xbc.shape
    shift = _conv_shift_matrix()
    rows = CHUNK * CHUNKS_PER_STEP
    assert s % rows == 0
    nb = s // rows
    hb = rows // CONV_HALO
    n_halo = s // CONV_HALO
    row = lambda bi, c: (bi, c, 0)
    params = pltpu.CompilerParams(dimension_semantics=("parallel", "arbitrary"), vmem_limit_bytes=VMEM_LIMIT)
    xc, yf = pl.pallas_call(
        _ssd_fwd_kernel,
        grid=(b, nb),
        in_specs=[
            pl.BlockSpec((None, rows, CONV_DIM), row),
            pl.BlockSpec((None, CONV_HALO, CONV_DIM), lambda bi, c: (bi, jnp.maximum(c * hb - 1, 0), 0)),
            pl.BlockSpec((None, CONV_HALO, CONV_DIM), lambda bi, c: (bi, jnp.minimum((c + 1) * hb, n_halo - 1), 0)),
            pl.BlockSpec((None, rows, LANES), row),
            _const_spec(shift.shape), _const_spec(conv_w.shape), _const_spec(conv_b.shape),
            _const_spec(a_row.shape), _const_spec(dskip.shape), _const_spec(tri_lo.shape), _const_spec(e2_f.shape),
        ],
        out_specs=[pl.BlockSpec((None, rows, CONV_DIM), row), pl.BlockSpec((None, rows, D_INNER), row)],
        out_shape=[jax.ShapeDtypeStruct((b, s, CONV_DIM), BF16), jax.ShapeDtypeStruct((b, s, D_INNER), F32)],
        scratch_shapes=[pltpu.VMEM((SSD_N, D_INNER), F32)],
        compiler_params=params,
        name="ssd_fwd",
    )(xbc, xbc, xbc, dt, shift, conv_w, conv_b, a_row, dskip, tri_lo, e2_f)

    rev = lambda bi, c: (bi, nb - 1 - c, 0)
    return pl.pallas_call(
        _ssd_bwd_kernel,
        grid=(b, nb),
        in_specs=[
            pl.BlockSpec((None, rows, CONV_DIM), rev),
            pl.BlockSpec((None, rows, LANES), rev),
            pl.BlockSpec((None, rows, D_INNER), rev),
            pl.BlockSpec((None, rows, D_INNER), rev),
            _const_spec(a_row.shape), _const_spec(ssd_norm_w.shape), _const_spec(tri_up.shape),
            _const_spec(e2_b.shape),
        ],
        out_specs=pl.BlockSpec((None, rows, D_INNER), rev),
        out_shape=jax.ShapeDtypeStruct((b, s, D_INNER), BF16),
        scratch_shapes=[pltpu.VMEM((SSD_N, D_INNER), F32), pltpu.VMEM((rows, D_INNER), F32)],
        compiler_params=params,
        name="ssd_bwd",
    )(xc, dt, yf, zs, a_row, ssd_norm_w, tri_up, e2_b)


def _merge_kernel(attn_ref, ssd_ref, g_ref, x_ref, gate1_ref, shift2_ref, scale2_ref, nw2_ref,
                  wao_ref, wso_ref, wo_ref, x1_out, h2_out):
    a = _dot(attn_ref[...], wao_ref[...])
    sm = _dot(ssd_ref[...], wso_ref[...])
    g = g_ref[...].astype(F32)
    merged = (g[:, :D_MODEL] * a + g[:, D_MODEL:] * sm).astype(BF16)
    x1 = x_ref[...] + gate1_ref[...] * _dot(merged, wo_ref[...])
    x1_out[...] = x1
    ms = jnp.mean(x1 * x1, axis=-1, keepdims=True)
    xn = x1 * lax.rsqrt(ms + EPS)
    h2_out[...] = (xn * (nw2_ref[...] * (1.0 + scale2_ref[...])) + shift2_ref[...]).astype(BF16)


def _merge(attn, ssd, gates, x, gate1, shift2, scale2, norm2_w, wao, wso, wo, tm):
    b, s, d = x.shape
    row = lambda bi, i: (bi, i, 0)
    per_b = pl.BlockSpec((None, 1, d), lambda bi, i: (bi, 0, 0))
    return pl.pallas_call(
        _merge_kernel,
        grid=(b, s // tm),
        in_specs=[
            pl.BlockSpec((None, tm, attn.shape[-1]), row), pl.BlockSpec((None, tm, D_INNER), row),
            pl.BlockSpec((None, tm, 2 * d), row), pl.BlockSpec((None, tm, d), row),
            per_b, per_b, per_b, _const_spec((1, d)),
            _const_spec(wao.shape), _const_spec(wso.shape), _const_spec(wo.shape),
        ],
        out_specs=[pl.BlockSpec((None, tm, d), row), pl.BlockSpec((None, tm, d), row)],
        out_shape=[jax.ShapeDtypeStruct((b, s, d), F32), jax.ShapeDtypeStruct((b, s, d), BF16)],
        compiler_params=pltpu.CompilerParams(dimension_semantics=("parallel", "parallel"),
                                             vmem_limit_bytes=VMEM_LIMIT),
        name="merge",
    )(attn, ssd, gates, x, gate1, shift2, scale2, norm2_w, wao, wso, wo)


def _mlp_kernel(h2_ref, x1_ref, gate2_ref, w1_ref, w2_ref, o_ref):
    h2 = h2_ref[...]
    acc = jnp.zeros(x1_ref.shape, F32)
    for blk in range(D_FF // 1024):
        sl = slice(blk * 1024, (blk + 1) * 1024)
        u = jnp.maximum(_dot(h2, w1_ref[:, sl]), 0.0)
        acc = acc + _dot((u * u).astype(BF16), w2_ref[sl, :])
    o_ref[...] = x1_ref[...] + gate2_ref[...] * acc


def _mlp(h2, x1, gate2, w1, w2, tm):
    b, s, d = x1.shape
    row = lambda bi, i: (bi, i, 0)
    return pl.pallas_call(
        _mlp_kernel,
        grid=(b, s // tm),
        in_specs=[
            pl.BlockSpec((None, tm, d), row), pl.BlockSpec((None, tm, d), row),
            pl.BlockSpec((None, 1, d), lambda bi, i: (bi, 0, 0)),
            _const_spec(w1.shape), _const_spec(w2.shape),
        ],
        out_specs=pl.BlockSpec((None, tm, d), row),
        out_shape=jax.ShapeDtypeStruct((b, s, d), F32),
        compiler_params=pltpu.CompilerParams(dimension_semantics=("parallel", "parallel"),
                                             vmem_limit_bytes=VMEM_LIMIT),
        name="mlp",
    )(h2, x1, gate2, w1, w2)


def _swap_halves(w):
    return w.reshape(2, 2, 16)[:, ::-1, :].reshape(HEAD_DIM)


def _rope_tables(s, norm_w, scale):
    pos = jnp.arange(s, dtype=jnp.int32)
    pos_row = (pos // GRID_W).astype(F32)
    pos_col = (pos % GRID_W).astype(F32)
    axis_dim = HEAD_DIM // 2
    inv_freq = ROPE_THETA ** (-jnp.arange(0, axis_dim, 2, dtype=F32) / axis_dim)
    ang_r = pos_row[:, None] * inv_freq[None, :]
    ang_c = pos_col[:, None] * inv_freq[None, :]
    cos = jnp.concatenate([jnp.cos(ang_r)] * 2 + [jnp.cos(ang_c)] * 2, axis=-1)
    sin_r, sin_c = jnp.sin(ang_r), jnp.sin(ang_c)
    sin = jnp.concatenate([-sin_r, sin_r, -sin_c, sin_c], axis=-1)
    ta = cos * norm_w[None, :] * scale
    tb = sin * _swap_halves(norm_w)[None, :] * scale
    return jnp.tile(ta, (1, 2)), jnp.tile(tb, (1, 2))


def _head_expansion(slot0):
    e = np.zeros((LANES, D_INNER), np.float32)
    for h in range(SSD_H):
        e[slot0 + h, h * SSD_P:(h + 1) * SSD_P] = 1.0
    return jnp.asarray(np.concatenate([e, e], axis=0), BF16)


def _pick_tile(s, pref):
    t = min(pref, s)
    assert s % t == 0
    return t


def kernel(x, c, w_ada, b_ada, norm1_w, norm2_w, w_in, q_norm_w, k_norm_w, conv_w, conv_b, A_log, dt_bias,
           ssd_D, ssd_norm_w, w_attn_out, w_ssd_out, w_o, w_mlp1, w_mlp2):
    b, s, d = x.shape
    depth = w_ada.shape[0]
    assert d == D_MODEL and s % (CHUNK * CHUNKS_PER_STEP) == 0 and s % GRID_W == 0

    qdim = N_Q_HEADS * HEAD_DIM
    kvdim = N_KV_HEADS * HEAD_DIM
    o_k, o_v, o_x = qdim, qdim + kvdim, qdim + 2 * kvdim
    o_z = o_x + CONV_DIM
    o_dt = o_z + D_INNER
    o_g = o_dt + 2 * SSD_H

    gsum = jnp.asarray(np.kron(np.eye(4, dtype=np.float32), np.full((HEAD_DIM, HEAD_DIM), 1.0 / HEAD_DIM,
                                                                       np.float32)), BF16)
    tri = np.tril(np.ones((CHUNK, CHUNK), np.float32))
    tri_lo = jnp.asarray(tri, BF16)
    tri_up = jnp.asarray(tri.T, BF16)
    e2_f = _head_expansion(0)
    e2_b = _head_expansion(SSD_H)
    ts = _pick_tile(s, 512)
    tq = _pick_tile(s, 128)
    tk = _pick_tile(s, 256)

    for l in range(depth):
        mod = _ada(c, w_ada[l], b_ada[l])
        shift1, scale1, gate1, shift2, scale2, gate2 = [m[:, None, :] for m in jnp.split(mod, 6, axis=-1)]

        w = w_in[l]
        wq = w[:, :o_k].astype(BF16)
        wk = w[:, o_k:o_v].astype(BF16)
        wv = w[:, o_v:o_x].T.astype(BF16)
        wx = w[:, o_x:o_z].astype(BF16)
        wz = w[:, o_z:o_dt].astype(BF16)
        wdt = jnp.pad(w[:, o_dt:o_g], ((0, 0), (0, LANES - 2 * SSD_H))).astype(BF16)
        wg = w[:, o_g:].astype(BF16)
        dtb = jnp.pad(dt_bias[l].reshape(1, 2 * SSD_H), ((0, 0), (0, LANES - 2 * SSD_H)))
        q_scale = math.log2(math.e) * HEAD_DIM ** -0.5
        qa, qb = _rope_tables(s, q_norm_w[l], q_scale)
        ka, kb = _rope_tables(s, k_norm_w[l], 1.0)

        qt, k, vt, xbc, zs, dt, gates = _inproj(x, shift1, scale1, norm1_w[l].reshape(1, d), wq, wk, wv, wx, wz,
                                                wdt, wg, gsum, qa, qb, ka, kb, dtb, ts)
        score_bound = (q_scale * HEAD_DIM * jnp.max(jnp.abs(q_norm_w[l])) * jnp.max(jnp.abs(k_norm_w[l])))
        attn = lax.cond(score_bound <= MAX_UNSHIFTED_SCORE,
                        lambda *a: _attention(*a, tq, tk, _attn_unshifted_kernel),
                        lambda *a: _attention(*a, tq, tk, _attn_kernel), qt, k, vt)

        a_row = jnp.pad(-jnp.exp(A_log[l].astype(F32)).reshape(1, 2 * SSD_H), ((0, 0), (0, LANES - 2 * SSD_H)))
        dskip = jnp.repeat(ssd_D[l].astype(F32), SSD_P).reshape(1, D_INNER)
        ssd = _ssd(xbc, dt, zs, conv_w[l], conv_b[l].reshape(1, CONV_DIM), a_row, dskip,
                   ssd_norm_w[l].reshape(1, D_INNER), tri_lo, tri_up, e2_f, e2_b)

        x1, h2 = _merge(attn, ssd, gates, x, gate1, shift2, scale2, norm2_w[l].reshape(1, d),
                        w_attn_out[l].astype(BF16), w_ssd_out[l].astype(BF16), w_o[l].astype(BF16), ts)
        x = _mlp(h2, x1, gate2, w_mlp1[l].astype(BF16), w_mlp2[l].astype(BF16), ts)
    return x
```

```python
import functools
import math

import jax
import jax.numpy as jnp
import numpy as np
from jax import lax
from jax.experimental import pallas as pl
from jax.experimental.pallas import tpu as pltpu

F32 = jnp.float32
BF16 = jnp.bfloat16

D_MODEL = 1024
GRID_W = 64
N_Q_HEADS = 16
N_KV_HEADS = 4
REP = N_Q_HEADS // N_KV_HEADS
HEAD_DIM = 64
ROPE_THETA = 10000.0
D_INNER = 2048
SSD_P = 64
SSD_H = D_INNER // SSD_P
SSD_G = 4
SSD_N = 128
HEADS_PER_GROUP = SSD_H // SSD_G
D_CONV = 5
CHUNK = 128
CONV_DIM = D_INNER + 2 * SSD_G * SSD_N
D_FF = 4 * D_MODEL
EPS = 1e-6

LANES = 128
V_ROWS = 80
CONV_HALO = 16
CHUNKS_PER_STEP = 2
SHIFT_K = 256
NEG_BIG = -1e30
MAX_UNSHIFTED_SCORE = 60.0
VMEM_LIMIT = 56 * 1024 * 1024


def _dot(a, b):
    return jnp.dot(a, b, preferred_element_type=F32)


def _dot_nt(a, b):
    return lax.dot_general(a, b, (((1,), (1,)), ((), ())), preferred_element_type=F32)


def _split3(v):
    hi = v.astype(BF16)
    r = v - hi.astype(F32)
    mid = r.astype(BF16)
    lo = (r - mid.astype(F32)).astype(BF16)
    return hi, mid, lo


def _const_spec(shape):
    return pl.BlockSpec(shape, lambda *_: (0,) * len(shape))


def _ada_kernel(c_ref, w_ref, b_ref, o_ref):
    c = c_ref[...]
    sc = c * jax.nn.sigmoid(c)
    a_hi, a_mid, a_lo = _split3(sc)
    w_hi, w_mid, w_lo = _split3(w_ref[...])
    acc = _dot(a_hi, w_hi)
    acc += _dot(a_hi, w_mid) + _dot(a_mid, w_hi)
    acc += _dot(a_hi, w_lo) + _dot(a_mid, w_mid) + _dot(a_lo, w_hi)
    o_ref[...] = acc + b_ref[...]


def _ada(c, w_ada, b_ada):
    b, d = c.shape
    n = w_ada.shape[1]
    tn = 1536
    return pl.pallas_call(
        _ada_kernel,
        grid=(n // tn,),
        in_specs=[
            pl.BlockSpec((b, d), lambda j: (0, 0)),
            pl.BlockSpec((d, tn), lambda j: (0, j)),
            pl.BlockSpec((1, tn), lambda j: (0, j)),
        ],
        out_specs=pl.BlockSpec((b, tn), lambda j: (0, j)),
        out_shape=jax.ShapeDtypeStruct((b, n), F32),
        compiler_params=pltpu.CompilerParams(dimension_semantics=("arbitrary",), vmem_limit_bytes=VMEM_LIMIT),
        name="ada",
    )(c, w_ada, b_ada.reshape(1, n))


def _norm_rope_block(t, ss, ta, tb, swap_lo):
    n = t * lax.rsqrt(ss + EPS)
    fwd = pltpu.roll(n, LANES - 16, axis=1)
    bwd = pltpu.roll(n, 16, axis=1)
    sw = jnp.where(swap_lo, fwd, bwd)
    return n * ta + sw * tb


def _inproj_kernel(x_ref, shift_ref, scale_ref, nw_ref, wq_ref, wk_ref, wv_ref, wx_ref, wz_ref, wdt_ref, wg_ref,
                   gsum_ref, qa_ref, qb_ref, ka_ref, kb_ref, dtb_ref,
                   q_out, k_out, vt_out, xbc_out, z_out, dt_out, g_out):
    x = x_ref[...]
    ts = x.shape[0]
    ms = jnp.mean(x * x, axis=-1, keepdims=True)
    xn = x * lax.rsqrt(ms + EPS)
    h = (xn * (nw_ref[...] * (1.0 + scale_ref[...])) + shift_ref[...]).astype(BF16)

    lane = lax.broadcasted_iota(jnp.int32, (ts, LANES), 1)
    swap_lo = (lane & 31) < 16
    gsum = gsum_ref[...]

    qf = _dot(h, wq_ref[...])
    qa = qa_ref[...]
    qb = qb_ref[...]
    for blk in range(N_Q_HEADS * HEAD_DIM // 256):
        t = qf[:, blk * 256:(blk + 1) * 256]
        ss = _dot((t * t).astype(BF16), gsum)
        for half in range(2):
            sl = slice(half * LANES, (half + 1) * LANES)
            rt = _norm_rope_block(t[:, sl], ss[:, sl], qa, qb, swap_lo).T.astype(BF16)
            hd = blk * 4 + half * 2
            q_out[hd] = rt[:HEAD_DIM, :]
            q_out[hd + 1] = rt[HEAD_DIM:, :]

    kf = _dot(h, wk_ref[...])
    ssk = _dot((kf * kf).astype(BF16), gsum)
    ka = ka_ref[...]
    kb = kb_ref[...]
    for half in range(2):
        sl = slice(half * LANES, (half + 1) * LANES)
        r = _norm_rope_block(kf[:, sl], ssk[:, sl], ka, kb, swap_lo).astype(BF16)
        k_out[2 * half] = r[:, :HEAD_DIM]
        k_out[2 * half + 1] = r[:, HEAD_DIM:]

    vt = _dot_nt(wv_ref[...], h).astype(BF16)
    ones_rows = (lax.broadcasted_iota(jnp.int32, (V_ROWS - HEAD_DIM, ts), 0) == 0).astype(BF16)
    for g in range(N_KV_HEADS):
        vt_out[g, :HEAD_DIM, :] = vt[g * HEAD_DIM:(g + 1) * HEAD_DIM, :]
        vt_out[g, HEAD_DIM:, :] = ones_rows

    for blk in range(CONV_DIM // 512):
        sl = slice(blk * 512, (blk + 1) * 512)
        xbc_out[:, sl] = _dot(h, wx_ref[:, sl]).astype(BF16)

    for blk in range(D_INNER // 512):
        sl = slice(blk * 512, (blk + 1) * 512)
        zf = _dot(h, wz_ref[:, sl])
        z_out[:, sl] = (zf * jax.nn.sigmoid(zf)).astype(BF16)

    dtf = _dot(h, wdt_ref[...]) + dtb_ref[...]
    dt_out[...] = jnp.maximum(dtf, 0.0) + jnp.log1p(jnp.exp(-jnp.abs(dtf)))

    for blk in range(2 * D_MODEL // 512):
        sl = slice(blk * 512, (blk + 1) * 512)
        g_out[:, sl] = jax.nn.sigmoid(_dot(h, wg_ref[:, sl])).astype(BF16)


def _inproj(x, shift1, scale1, norm1_w, wq, wk, wv, wx, wz, wdt, wg, gsum, qa, qb, ka, kb, dtb, ts):
    b, s, d = x.shape
    n_t = s // ts
    row = lambda bi, i: (bi, i, 0)
    per_b = pl.BlockSpec((None, 1, d), lambda bi, i: (bi, 0, 0))
    tab = pl.BlockSpec((ts, LANES), lambda bi, i: (i, 0))
    in_specs = [
        pl.BlockSpec((None, ts, d), row), per_b, per_b, _const_spec((1, d)),
        _const_spec(wq.shape), _const_spec(wk.shape), _const_spec(wv.shape), _const_spec(wx.shape),
        _const_spec(wz.shape), _const_spec(wdt.shape), _const_spec(wg.shape), _const_spec(gsum.shape),
        tab, tab, tab, tab, _const_spec(dtb.shape),
    ]
    out_specs = [
        pl.BlockSpec((None, N_Q_HEADS, HEAD_DIM, ts), lambda bi, i: (bi, 0, 0, i)),
        pl.BlockSpec((None, N_KV_HEADS, ts, HEAD_DIM), lambda bi, i: (bi, 0, i, 0)),
        pl.BlockSpec((None, N_KV_HEADS, V_ROWS, ts), lambda bi, i: (bi, 0, 0, i)),
        pl.BlockSpec((None, ts, CONV_DIM), row),
        pl.BlockSpec((None, ts, D_INNER), row),
        pl.BlockSpec((None, ts, LANES), row),
        pl.BlockSpec((None, ts, 2 * D_MODEL), row),
    ]
    out_shape = [
        jax.ShapeDtypeStruct((b, N_Q_HEADS, HEAD_DIM, s), BF16),
        jax.ShapeDtypeStruct((b, N_KV_HEADS, s, HEAD_DIM), BF16),
        jax.ShapeDtypeStruct((b, N_KV_HEADS, V_ROWS, s), BF16),
        jax.ShapeDtypeStruct((b, s, CONV_DIM), BF16),
        jax.ShapeDtypeStruct((b, s, D_INNER), BF16),
        jax.ShapeDtypeStruct((b, s, LANES), F32),
        jax.ShapeDtypeStruct((b, s, 2 * D_MODEL), BF16),
    ]
    return pl.pallas_call(
        _inproj_kernel,
        grid=(b, n_t),
        in_specs=in_specs,
        out_specs=out_specs,
        out_shape=out_shape,
        compiler_params=pltpu.CompilerParams(dimension_semantics=("parallel", "parallel"),
                                             vmem_limit_bytes=VMEM_LIMIT),
        name="in_proj",
    )(x, shift1, scale1, norm1_w, wq, wk, wv, wx, wz, wdt, wg, gsum, qa, qb, ka, kb, dtb)


def _load_qt(qt_ref):
    return jnp.concatenate([qt_ref[r] for r in range(REP)], axis=1)


def _finish_attn(acc, o_ref, tq):
    o_t = acc[:HEAD_DIM] / acc[HEAD_DIM:HEAD_DIM + 1]
    o4 = jnp.concatenate([o_t[:, r * tq:(r + 1) * tq] for r in range(REP)], axis=0)
    o_ref[...] = o4.T.astype(o_ref.dtype)


def _attn_kernel(qt_ref, k_ref, vt_ref, o_ref, *, tq, tk):
    s_len = k_ref.shape[0]
    cols = REP * tq
    qt = _load_qt(qt_ref)

    def body(j, carry):
        m, acc = carry
        off = pl.multiple_of(j * tk, tk)
        sc = _dot(k_ref[pl.ds(off, tk), :], qt)
        m_new = jnp.maximum(m, jnp.max(sc, axis=0, keepdims=True))
        p = jnp.exp2(sc - m_new)
        alpha = jnp.exp2(m - m_new)
        acc = alpha * acc + _dot(vt_ref[:, pl.ds(off, tk)], p.astype(BF16))
        return m_new, acc

    m0 = jnp.full((1, cols), -jnp.inf, F32)
    acc0 = jnp.zeros((V_ROWS, cols), F32)
    _, acc = lax.fori_loop(0, s_len // tk, body, (m0, acc0))
    _finish_attn(acc, o_ref, tq)


def _attn_unshifted_kernel(qt_ref, k_ref, vt_ref, o_ref, *, tq, tk):
    s_len = k_ref.shape[0]
    nk = s_len // tk
    cols = REP * tq
    qt = _load_qt(qt_ref)

    def probs(j):
        off = pl.multiple_of(j * tk, tk)
        return jnp.exp2(_dot(k_ref[pl.ds(off, tk), :], qt)).astype(BF16)

    def pv(p, j):
        off = pl.multiple_of(j * tk, tk)
        return _dot(vt_ref[:, pl.ds(off, tk)], p)

    def body(j, carry):
        p, acc = carry
        p_next = probs(j + 1)
        return p_next, acc + pv(p, j)

    p, acc = lax.fori_loop(0, nk - 1, body, (probs(0), jnp.zeros((V_ROWS, cols), F32)), unroll=True)
    _finish_attn(acc + pv(p, nk - 1), o_ref, tq)


def _attention(qt, k, vt, tq, tk, body):
    b, _, _, s = qt.shape
    return pl.pallas_call(
        functools.partial(body, tq=tq, tk=tk),
        grid=(b, N_KV_HEADS, s // tq),
        in_specs=[
            pl.BlockSpec((None, REP, HEAD_DIM, tq), lambda bi, g, i: (bi, g, 0, i)),
            pl.BlockSpec((None, None, s, HEAD_DIM), lambda bi, g, i: (bi, g, 0, 0)),
            pl.BlockSpec((None, None, V_ROWS, s), lambda bi, g, i: (bi, g, 0, 0)),
        ],
        out_specs=pl.BlockSpec((None, tq, REP * HEAD_DIM), lambda bi, g, i: (bi, i, g)),
        out_shape=jax.ShapeDtypeStruct((b, s, N_Q_HEADS * HEAD_DIM), BF16),
        compiler_params=pltpu.CompilerParams(dimension_semantics=("parallel", "parallel", "arbitrary"),
                                             vmem_limit_bytes=VMEM_LIMIT),
        name="attn_online" if body is _attn_kernel else "attn_unshifted",
    )(qt, k, vt)


def _hi_mid(v):
    hi = v.astype(BF16)
    mid = (v - hi.astype(F32)).astype(BF16)
    return jnp.concatenate([hi, mid], axis=1)


def _ssd_scan_chunk(xs, bmat, cmat, dt, a_row, tmat, e2_ref, st_ref, emit, *, slot0, reverse):
    L = xs.shape[0]
    a = dt * a_row
    a_hi, a_mid, a_lo = _split3(a)
    cum = _dot(tmat, a_hi) + _dot(tmat, a_mid) + _dot(tmat, a_lo)
    total = cum[0:1, :] if reverse else cum[L - 1:L, :]
    src_t = (cum - jnp.log(dt)).T
    expand_lhs = _hi_mid(jnp.concatenate([jnp.exp(total - cum) * dt, jnp.exp(cum),
                                          jnp.broadcast_to(jnp.exp(total), (8, LANES))], axis=0))

    li = lax.broadcasted_iota(jnp.int32, (L, L), 0)
    si = lax.broadcasted_iota(jnp.int32, (L, L), 1)
    keep = (si >= li) if reverse else (li >= si)
    lane = lax.broadcasted_iota(jnp.int32, (L, LANES), 1)
    first_head = lane < SSD_P

    for g in range(SSD_G):
        gs = slice(g * SSD_N, (g + 1) * SSD_N)
        cols = slice(g * HEADS_PER_GROUP * SSD_P, (g + 1) * HEADS_PER_GROUP * SSD_P)
        e2_g = e2_ref[:, cols]
        bg = bmat[:, gs]
        cg_bf = cmat[:, gs].astype(BF16)
        cb = _dot_nt(cg_bf, bg.astype(BF16))
        st_g = st_ref[:, cols]
        expanded = _dot(expand_lhs, e2_g)
        y_off = _dot(cg_bf, st_g.astype(BF16)) * expanded[L:2 * L, :]
        xw = (xs[:, cols] * expanded[:L, :]).astype(BF16)
        st_ref[:, cols] = expanded[2 * L:2 * L + 1, :] * st_g + _dot(bg.T.astype(BF16), xw)
        for pr in range(HEADS_PER_GROUP // 2):
            h0 = g * HEADS_PER_GROUP + 2 * pr
            ms = []
            for hh in (h0, h0 + 1):
                j = slot0 + hh
                arg = cum[:, j:j + 1] - src_t[j:j + 1, :]
                ms.append((cb * jnp.exp(jnp.where(keep, arg, NEG_BIG))).astype(BF16))
            lhs = jnp.concatenate(ms, axis=1)
            pc = slice(h0 * SSD_P, h0 * SSD_P + LANES)
            xp = xs[:, pc]
            rhs = jnp.concatenate([jnp.where(first_head, xp, 0.0).astype(BF16),
                                   jnp.where(first_head, 0.0, xp).astype(BF16)], axis=0)
            emit(pc, _dot(lhs, rhs) + y_off[:, pr * LANES:(pr + 1) * LANES])


def _ssd_fwd_kernel(x_ref, xprev_ref, xnext_ref, dt_ref, shift_ref, cw_ref, cb_ref, a_ref, dskip_ref, tmat_ref,
                    e2_ref, xc_out, y_out, st_ref):
    c = pl.program_id(1)
    nc = pl.num_programs(1)
    L = CHUNK

    @pl.when(c == 0)
    def _():
        st_ref[...] = jnp.zeros_like(st_ref)

    zero_halo = jnp.zeros((CONV_HALO, CONV_DIM), BF16)
    before = jnp.where(c > 0, xprev_ref[...], zero_halo)
    after = jnp.where(c < nc - 1, xnext_ref[...], zero_halo)
    shift = shift_ref[...]
    mid_tap = (D_CONV - 1) // 2
    taps = [k for k in range(D_CONV) if k != mid_tap]
    a_row = a_ref[...]
    tmat = tmat_ref[...]

    for sub in range(CHUNKS_PER_STEP):
        r0 = sub * L
        rows = slice(r0, r0 + L)
        x_mid = x_ref[rows, :]
        prev = before if sub == 0 else x_ref[r0 - CONV_HALO:r0, :]
        nxt = after if sub == CHUNKS_PER_STEP - 1 else x_ref[r0 + L:r0 + L + CONV_HALO, :]
        ext = jnp.concatenate([prev, x_mid, nxt, jnp.zeros((SHIFT_K - L - 2 * CONV_HALO, CONV_DIM), BF16)], axis=0)
        acts = []
        for blk in range(CONV_DIM // 512):
            sl = slice(blk * 512, (blk + 1) * 512)
            shifted = _dot(shift, ext[:, sl])
            acc = cb_ref[:, sl] + cw_ref[mid_tap:mid_tap + 1, sl] * x_mid[:, sl].astype(F32)
            for i, k in enumerate(taps):
                acc = acc + cw_ref[k:k + 1, sl] * shifted[i * L:(i + 1) * L, :]
            act = acc * jax.nn.sigmoid(acc)
            xc_out[rows, sl] = act.astype(BF16)
            acts.append(act)
        xs = jnp.concatenate(acts[:D_INNER // 512], axis=1)
        bmat = acts[D_INNER // 512]
        cmat = acts[D_INNER // 512 + 1]

        def emit(pc, y, rows=rows, xs=xs):
            y_out[rows, pc] = y + dskip_ref[:, pc] * xs[:, pc]

        _ssd_scan_chunk(xs, bmat, cmat, dt_ref[rows, :], a_row, tmat, e2_ref, st_ref, emit, slot0=0, reverse=False)


def _ssd_bwd_kernel(xc_ref, dt_ref, yf_ref, zs_ref, a_ref, nw_ref, tmat_ref, e2_ref, o_ref, st_ref, y_ref):
    c = pl.program_id(1)
    L = CHUNK

    @pl.when(c == 0)
    def _():
        st_ref[...] = jnp.zeros_like(st_ref)

    a_row = a_ref[...]
    tmat = tmat_ref[...]
    for sub in reversed(range(CHUNKS_PER_STEP)):
        rows = slice(sub * L, (sub + 1) * L)
        xc = xc_ref[rows, :].astype(F32)
        xs = xc[:, :D_INNER]
        bmat = xc[:, D_INNER:D_INNER + SSD_G * SSD_N]
        cmat = xc[:, D_INNER + SSD_G * SSD_N:]
        sq = []

        def emit(pc, y, rows=rows, sq=sq):
            gated = (y + yf_ref[rows, pc]) * zs_ref[rows, pc].astype(F32)
            y_ref[rows, pc] = gated
            sq.append(gated * gated)

        _ssd_scan_chunk(xs, bmat, cmat, dt_ref[rows, :], a_row, tmat, e2_ref, st_ref, emit, slot0=SSD_H,
                        reverse=True)
        ms = jnp.sum(functools.reduce(lambda u, v: u + v, sq), axis=-1, keepdims=True) * (1.0 / D_INNER)
        o_ref[rows, :] = (y_ref[rows, :] * lax.rsqrt(ms + EPS) * nw_ref[...]).astype(o_ref.dtype)


def _conv_shift_matrix():
    pad = (D_CONV - 1) // 2
    taps = [k for k in range(D_CONV) if k != pad]
    m = np.zeros((len(taps) * CHUNK, SHIFT_K), np.float32)
    for i, k in enumerate(taps):
        for l in range(CHUNK):
            m[i * CHUNK + l, CONV_HALO + l + k - pad] = 1.0
    return jnp.asarray(m, BF16)


def _ssd(xbc, dt, zs, conv_w, conv_b, a_row, dskip, ssd_norm_w, tri_lo, tri_up, e2_f, e2_b):
    b, s, _ = xbc.shape
    shift = _conv_shift_matrix()
    rows = CHUNK * CHUNKS_PER_STEP
    assert s % rows == 0
    nb = s // rows
    hb = rows // CONV_HALO
    n_halo = s // CONV_HALO
    row = lambda bi, c: (bi, c, 0)
    params = pltpu.CompilerParams(dimension_semantics=("parallel", "arbitrary"), vmem_limit_bytes=VMEM_LIMIT)
    xc, yf = pl.pallas_call(
        _ssd_fwd_kernel,
        grid=(b, nb),
        in_specs=[
            pl.BlockSpec((None, rows, CONV_DIM), row),
            pl.BlockSpec((None, CONV_HALO, CONV_DIM), lambda bi, c: (bi, jnp.maximum(c * hb - 1, 0), 0)),
            pl.BlockSpec((None, CONV_HALO, CONV_DIM), lambda bi, c: (bi, jnp.minimum((c + 1) * hb, n_halo - 1), 0)),
            pl.BlockSpec((None, rows, LANES), row),
            _const_spec(shift.shape), _const_spec(conv_w.shape), _const_spec(conv_b.shape),
            _const_spec(a_row.shape), _const_spec(dskip.shape), _const_spec(tri_lo.shape), _const_spec(e2_f.shape),
        ],
        out_specs=[pl.BlockSpec((None, rows, CONV_DIM), row), pl.BlockSpec((None, rows, D_INNER), row)],
        out_shape=[jax.ShapeDtypeStruct((b, s, CONV_DIM), BF16), jax.ShapeDtypeStruct((b, s, D_INNER), F32)],
        scratch_shapes=[pltpu.VMEM((SSD_N, D_INNER), F32)],
        compiler_params=params,
        name="ssd_fwd",
    )(xbc, xbc, xbc, dt, shift, conv_w, conv_b, a_row, dskip, tri_lo, e2_f)

    rev = lambda bi, c: (bi, nb - 1 - c, 0)
    return pl.pallas_call(
        _ssd_bwd_kernel,
        grid=(b, nb),
        in_specs=[
            pl.BlockSpec((None, rows, CONV_DIM), rev),
            pl.BlockSpec((None, rows, LANES), rev),
            pl.BlockSpec((None, rows, D_INNER), rev),
            pl.BlockSpec((None, rows, D_INNER), rev),
            _const_spec(a_row.shape), _const_spec(ssd_norm_w.shape), _const_spec(tri_up.shape),
            _const_spec(e2_b.shape),
        ],
        out_specs=pl.BlockSpec((None, rows, D_INNER), rev),
        out_shape=jax.ShapeDtypeStruct((b, s, D_INNER), BF16),
        scratch_shapes=[pltpu.VMEM((SSD_N, D_INNER), F32), pltpu.VMEM((rows, D_INNER), F32)],
        compiler_params=params,
        name="ssd_bwd",
    )(xc, dt, yf, zs, a_row, ssd_norm_w, tri_up, e2_b)


def _merge_kernel(attn_ref, ssd_ref, g_ref, x_ref, gate1_ref, shift2_ref, scale2_ref, nw2_ref,
                  wao_ref, wso_ref, wo_ref, x1_out, h2_out):
    a = _dot(attn_ref[...], wao_ref[...])
    sm = _dot(ssd_ref[...], wso_ref[...])
    g = g_ref[...].astype(F32)
    merged = (g[:, :D_MODEL] * a + g[:, D_MODEL:] * sm).astype(BF16)
    x1 = x_ref[...] + gate1_ref[...] * _dot(merged, wo_ref[...])
    x1_out[...] = x1
    ms = jnp.mean(x1 * x1, axis=-1, keepdims=True)
    xn = x1 * lax.rsqrt(ms + EPS)
    h2_out[...] = (xn * (nw2_ref[...] * (1.0 + scale2_ref[...])) + shift2_ref[...]).astype(BF16)


def _merge(attn, ssd, gates, x, gate1, shift2, scale2, norm2_w, wao, wso, wo, tm):
    b, s, d = x.shape
    row = lambda bi, i: (bi, i, 0)
    per_b = pl.BlockSpec((None, 1, d), lambda bi, i: (bi, 0, 0))
    return pl.pallas_call(
        _merge_kernel,
        grid=(b, s // tm),
        in_specs=[
            pl.BlockSpec((None, tm, attn.shape[-1]), row), pl.BlockSpec((None, tm, D_INNER), row),
            pl.BlockSpec((None, tm, 2 * d), row), pl.BlockSpec((None, tm, d), row),
            per_b, per_b, per_b, _const_spec((1, d)),
            _const_spec(wao.shape), _const_spec(wso.shape), _const_spec(wo.shape),
        ],
        out_specs=[pl.BlockSpec((None, tm, d), row), pl.BlockSpec((None, tm, d), row)],
        out_shape=[jax.ShapeDtypeStruct((b, s, d), F32), jax.ShapeDtypeStruct((b, s, d), BF16)],
        compiler_params=pltpu.CompilerParams(dimension_semantics=("parallel", "parallel"),
                                             vmem_limit_bytes=VMEM_LIMIT),
        name="merge",
    )(attn, ssd, gates, x, gate1, shift2, scale2, norm2_w, wao, wso, wo)


def _mlp_kernel(h2_ref, x1_ref, gate2_ref, w1_ref, w2_ref, o_ref):
    h2 = h2_ref[...]
    acc = jnp.zeros(x1_ref.shape, F32)
    for blk in range(D_FF // 1024):
        sl = slice(blk * 1024, (blk + 1) * 1024)
        u = jnp.maximum(_dot(h2, w1_ref[:, sl]), 0.0)
        acc = acc + _dot((u * u).astype(BF16), w2_ref[sl, :])
    o_ref[...] = x1_ref[...] + gate2_ref[...] * acc


def _mlp(h2, x1, gate2, w1, w2, tm):
    b, s, d = x1.shape
    row = lambda bi, i: (bi, i, 0)
    return pl.pallas_call(
        _mlp_kernel,
        grid=(b, s // tm),
        in_specs=[
            pl.BlockSpec((None, tm, d), row), pl.BlockSpec((None, tm, d), row),
            pl.BlockSpec((None, 1, d), lambda bi, i: (bi, 0, 0)),
            _const_spec(w1.shape), _const_spec(w2.shape),
        ],
        out_specs=pl.BlockSpec((None, tm, d), row),
        out_shape=jax.ShapeDtypeStruct((b, s, d), F32),
        compiler_params=pltpu.CompilerParams(dimension_semantics=("parallel", "parallel"),
                                             vmem_limit_bytes=VMEM_LIMIT),
        name="mlp",
    )(h2, x1, gate2, w1, w2)


def _swap_halves(w):
    return w.reshape(2, 2, 16)[:, ::-1, :].reshape(HEAD_DIM)


def _rope_tables(s, norm_w, scale):
    pos = jnp.arange(s, dtype=jnp.int32)
    pos_row = (pos // GRID_W).astype(F32)
    pos_col = (pos % GRID_W).astype(F32)
    axis_dim = HEAD_DIM // 2
    inv_freq = ROPE_THETA ** (-jnp.arange(0, axis_dim, 2, dtype=F32) / axis_dim)
    ang_r = pos_row[:, None] * inv_freq[None, :]
    ang_c = pos_col[:, None] * inv_freq[None, :]
    cos = jnp.concatenate([jnp.cos(ang_r)] * 2 + [jnp.cos(ang_c)] * 2, axis=-1)
    sin_r, sin_c = jnp.sin(ang_r), jnp.sin(ang_c)
    sin = jnp.concatenate([-sin_r, sin_r, -sin_c, sin_c], axis=-1)
    ta = cos * norm_w[None, :] * scale
    tb = sin * _swap_halves(norm_w)[None, :] * scale
    return jnp.tile(ta, (1, 2)), jnp.tile(tb, (1, 2))


def _head_expansion(slot0):
    e = np.zeros((LANES, D_INNER), np.float32)
    for h in range(SSD_H):
        e[slot0 + h, h * SSD_P:(h + 1) * SSD_P] = 1.0
    return jnp.asarray(np.concatenate([e, e], axis=0), BF16)


def _pick_tile(s, pref):
    t = min(pref, s)
    assert s % t == 0
    return t


def kernel(x, c, w_ada, b_ada, norm1_w, norm2_w, w_in, q_norm_w, k_norm_w, conv_w, conv_b, A_log, dt_bias,
           ssd_D, ssd_norm_w, w_attn_out, w_ssd_out, w_o, w_mlp1, w_mlp2):
    b, s, d = x.shape
    depth = w_ada.shape[0]
    assert d == D_MODEL and s % (CHUNK * CHUNKS_PER_STEP) == 0 and s % GRID_W == 0

    qdim = N_Q_HEADS * HEAD_DIM
    kvdim = N_KV_HEADS * HEAD_DIM
    o_k, o_v, o_x = qdim, qdim + kvdim, qdim + 2 * kvdim
    o_z = o_x + CONV_DIM
    o_dt = o_z + D_INNER
    o_g = o_dt + 2 * SSD_H

    gsum = jnp.asarray(np.kron(np.eye(4, dtype=np.float32), np.full((HEAD_DIM, HEAD_DIM), 1.0 / HEAD_DIM,
                                                                       np.float32)), BF16)
    tri = np.tril(np.ones((CHUNK, CHUNK), np.float32))
    tri_lo = jnp.asarray(tri, BF16)
    tri_up = jnp.asarray(tri.T, BF16)
    e2_f = _head_expansion(0)
    e2_b = _head_expansion(SSD_H)
    ts = _pick_tile(s, 512)
    tq = _pick_tile(s, 256)
    tk = _pick_tile(s, 256)

    for l in range(depth):
        mod = _ada(c, w_ada[l], b_ada[l])
        shift1, scale1, gate1, shift2, scale2, gate2 = [m[:, None, :] for m in jnp.split(mod, 6, axis=-1)]

        w = w_in[l]
        wq = w[:, :o_k].astype(BF16)
        wk = w[:, o_k:o_v].astype(BF16)
        wv = w[:, o_v:o_x].T.astype(BF16)
        wx = w[:, o_x:o_z].astype(BF16)
        wz = w[:, o_z:o_dt].astype(BF16)
        wdt = jnp.pad(w[:, o_dt:o_g], ((0, 0), (0, LANES - 2 * SSD_H))).astype(BF16)
        wg = w[:, o_g:].astype(BF16)
        dtb = jnp.pad(dt_bias[l].reshape(1, 2 * SSD_H), ((0, 0), (0, LANES - 2 * SSD_H)))
        q_scale = math.log2(math.e) * HEAD_DIM ** -0.5
        qa, qb = _rope_tables(s, q_norm_w[l], q_scale)
        ka, kb = _rope_tables(s, k_norm_w[l], 1.0)

        qt, k, vt, xbc, zs, dt, gates = _inproj(x, shift1, scale1, norm1_w[l].reshape(1, d), wq, wk, wv, wx, wz,
                                                wdt, wg, gsum, qa, qb, ka, kb, dtb, ts)
        score_bound = (q_scale * HEAD_DIM * jnp.max(jnp.abs(q_norm_w[l])) * jnp.max(jnp.abs(k_norm_w[l])))
        attn = lax.cond(score_bound <= MAX_UNSHIFTED_SCORE,
                        lambda *a: _attention(*a, tq, tk, _attn_unshifted_kernel),
                        lambda *a: _attention(*a, tq, tk, _attn_kernel), qt, k, vt)

        a_row = jnp.pad(-jnp.exp(A_log[l].astype(F32)).reshape(1, 2 * SSD_H), ((0, 0), (0, LANES - 2 * SSD_H)))
        dskip = jnp.repeat(ssd_D[l].astype(F32), SSD_P).reshape(1, D_INNER)
        ssd = _ssd(xbc, dt, zs, conv_w[l], conv_b[l].reshape(1, CONV_DIM), a_row, dskip,
                   ssd_norm_w[l].reshape(1, D_INNER), tri_lo, tri_up, e2_f, e2_b)

        x1, h2 = _merge(attn, ssd, gates, x, gate1, shift2, scale2, norm2_w[l].reshape(1, d),
                        w_attn_out[l].astype(BF16), w_ssd_out[l].astype(BF16), w_o[l].astype(BF16), ts)
        x = _mlp(h2, x1, gate2, w_mlp1[l].astype(BF16), w_mlp2[l].astype(BF16), ts)
    return x
```

```python
import functools
import math

import jax
import jax.numpy as jnp
import numpy as np
from jax import lax
from jax.experimental import pallas as pl
from jax.experimental.pallas import tpu as pltpu

F32 = jnp.float32
BF16 = jnp.bfloat16

D_MODEL = 1024
GRID_W = 64
N_Q_HEADS = 16
N_KV_HEADS = 4
REP = N_Q_HEADS // N_KV_HEADS
HEAD_DIM = 64
ROPE_THETA = 10000.0
D_INNER = 2048
SSD_P = 64
SSD_H = D_INNER // SSD_P
SSD_G = 4
SSD_N = 128
HEADS_PER_GROUP = SSD_H // SSD_G
D_CONV = 5
CHUNK = 128
CONV_DIM = D_INNER + 2 * SSD_G * SSD_N
D_FF = 4 * D_MODEL
EPS = 1e-6

LANES = 128
V_ROWS = 80
CONV_HALO = 16
CHUNKS_PER_STEP = 4
SHIFT_K = 256
NEG_BIG = -1e30
MAX_UNSHIFTED_SCORE = 60.0
VMEM_LIMIT = 56 * 1024 * 1024


def _dot(a, b):
    return jnp.dot(a, b, preferred_element_type=F32)


def _dot_nt(a, b):
    return lax.dot_general(a, b, (((1,), (1,)), ((), ())), preferred_element_type=F32)


def _split3(v):
    hi = v.astype(BF16)
    r = v - hi.astype(F32)
    mid = r.astype(BF16)
    lo = (r - mid.astype(F32)).astype(BF16)
    return hi, mid, lo


def _const_spec(shape):
    return pl.BlockSpec(shape, lambda *_: (0,) * len(shape))


def _ada_kernel(c_ref, w_ref, b_ref, o_ref):
    c = c_ref[...]
    sc = c * jax.nn.sigmoid(c)
    a_hi, a_mid, a_lo = _split3(sc)
    w_hi, w_mid, w_lo = _split3(w_ref[...])
    acc = _dot(a_hi, w_hi)
    acc += _dot(a_hi, w_mid) + _dot(a_mid, w_hi)
    acc += _dot(a_hi, w_lo) + _dot(a_mid, w_mid) + _dot(a_lo, w_hi)
    o_ref[...] = acc + b_ref[...]


def _ada(c, w_ada, b_ada):
    b, d = c.shape
    n = w_ada.shape[1]
    tn = 1536
    return pl.pallas_call(
        _ada_kernel,
        grid=(n // tn,),
        in_specs=[
            pl.BlockSpec((b, d), lambda j: (0, 0)),
            pl.BlockSpec((d, tn), lambda j: (0, j)),
            pl.BlockSpec((1, tn), lambda j: (0, j)),
        ],
        out_specs=pl.BlockSpec((b, tn), lambda j: (0, j)),
        out_shape=jax.ShapeDtypeStruct((b, n), F32),
        compiler_params=pltpu.CompilerParams(dimension_semantics=("arbitrary",), vmem_limit_bytes=VMEM_LIMIT),
        name="ada",
    )(c, w_ada, b_ada.reshape(1, n))


def _norm_rope_block(t, ss, ta, tb, swap_lo):
    n = t * lax.rsqrt(ss + EPS)
    fwd = pltpu.roll(n, LANES - 16, axis=1)
    bwd = pltpu.roll(n, 16, axis=1)
    sw = jnp.where(swap_lo, fwd, bwd)
    return n * ta + sw * tb


def _inproj_kernel(x_ref, shift_ref, scale_ref, nw_ref, wq_ref, wk_ref, wv_ref, wx_ref, wz_ref, wdt_ref, wg_ref,
                   gsum_ref, qa_ref, qb_ref, ka_ref, kb_ref, dtb_ref,
                   q_out, k_out, vt_out, xbc_out, z_out, dt_out, g_out):
    x = x_ref[...]
    ts = x.shape[0]
    ms = jnp.mean(x * x, axis=-1, keepdims=True)
    xn = x * lax.rsqrt(ms + EPS)
    h = (xn * (nw_ref[...] * (1.0 + scale_ref[...])) + shift_ref[...]).astype(BF16)

    lane = lax.broadcasted_iota(jnp.int32, (ts, LANES), 1)
    swap_lo = (lane & 31) < 16
    gsum = gsum_ref[...]

    qf = _dot(h, wq_ref[...])
    qa = qa_ref[...]
    qb = qb_ref[...]
    for blk in range(N_Q_HEADS * HEAD_DIM // 256):
        t = qf[:, blk * 256:(blk + 1) * 256]
        ss = _dot((t * t).astype(BF16), gsum)
        for half in range(2):
            sl = slice(half * LANES, (half + 1) * LANES)
            rt = _norm_rope_block(t[:, sl], ss[:, sl], qa, qb, swap_lo).T.astype(BF16)
            hd = blk * 4 + half * 2
            q_out[hd] = rt[:HEAD_DIM, :]
            q_out[hd + 1] = rt[HEAD_DIM:, :]

    kf = _dot(h, wk_ref[...])
    ssk = _dot((kf * kf).astype(BF16), gsum)
    ka = ka_ref[...]
    kb = kb_ref[...]
    for half in range(2):
        sl = slice(half * LANES, (half + 1) * LANES)
        r = _norm_rope_block(kf[:, sl], ssk[:, sl], ka, kb, swap_lo).astype(BF16)
        k_out[2 * half] = r[:, :HEAD_DIM]
        k_out[2 * half + 1] = r[:, HEAD_DIM:]

    vt = _dot_nt(wv_ref[...], h).astype(BF16)
    ones_rows = (lax.broadcasted_iota(jnp.int32, (V_ROWS - HEAD_DIM, ts), 0) == 0).astype(BF16)
    for g in range(N_KV_HEADS):
        vt_out[g, :HEAD_DIM, :] = vt[g * HEAD_DIM:(g + 1) * HEAD_DIM, :]
        vt_out[g, HEAD_DIM:, :] = ones_rows

    for blk in range(CONV_DIM // 512):
        sl = slice(blk * 512, (blk + 1) * 512)
        xbc_out[:, sl] = _dot(h, wx_ref[:, sl]).astype(BF16)

    for blk in range(D_INNER // 512):
        sl = slice(blk * 512, (blk + 1) * 512)
        zf = _dot(h, wz_ref[:, sl])
        z_out[:, sl] = (zf * jax.nn.sigmoid(zf)).astype(BF16)

    dtf = _dot(h, wdt_ref[...]) + dtb_ref[...]
    dt_out[...] = jnp.maximum(dtf, 0.0) + jnp.log1p(jnp.exp(-jnp.abs(dtf)))

    for blk in range(2 * D_MODEL // 512):
        sl = slice(blk * 512, (blk + 1) * 512)
        g_out[:, sl] = jax.nn.sigmoid(_dot(h, wg_ref[:, sl])).astype(BF16)


def _inproj(x, shift1, scale1, norm1_w, wq, wk, wv, wx, wz, wdt, wg, gsum, qa, qb, ka, kb, dtb, ts):
    b, s, d = x.shape
    n_t = s // ts
    row = lambda bi, i: (bi, i, 0)
    per_b = pl.BlockSpec((None, 1, d), lambda bi, i: (bi, 0, 0))
    tab = pl.BlockSpec((ts, LANES), lambda bi, i: (i, 0))
    in_specs = [
        pl.BlockSpec((None, ts, d), row), per_b, per_b, _const_spec((1, d)),
        _const_spec(wq.shape), _const_spec(wk.shape), _const_spec(wv.shape), _const_spec(wx.shape),
        _const_spec(wz.shape), _const_spec(wdt.shape), _const_spec(wg.shape), _const_spec(gsum.shape),
        tab, tab, tab, tab, _const_spec(dtb.shape),
    ]
    out_specs = [
        pl.BlockSpec((None, N_Q_HEADS, HEAD_DIM, ts), lambda bi, i: (bi, 0, 0, i)),
        pl.BlockSpec((None, N_KV_HEADS, ts, HEAD_DIM), lambda bi, i: (bi, 0, i, 0)),
        pl.BlockSpec((None, N_KV_HEADS, V_ROWS, ts), lambda bi, i: (bi, 0, 0, i)),
        pl.BlockSpec((None, ts, CONV_DIM), row),
        pl.BlockSpec((None, ts, D_INNER), row),
        pl.BlockSpec((None, ts, LANES), row),
        pl.BlockSpec((None, ts, 2 * D_MODEL), row),
    ]
    out_shape = [
        jax.ShapeDtypeStruct((b, N_Q_HEADS, HEAD_DIM, s), BF16),
        jax.ShapeDtypeStruct((b, N_KV_HEADS, s, HEAD_DIM), BF16),
        jax.ShapeDtypeStruct((b, N_KV_HEADS, V_ROWS, s), BF16),
        jax.ShapeDtypeStruct((b, s, CONV_DIM), BF16),
        jax.ShapeDtypeStruct((b, s, D_INNER), BF16),
        jax.ShapeDtypeStruct((b, s, LANES), F32),
        jax.ShapeDtypeStruct((b, s, 2 * D_MODEL), BF16),
    ]
    return pl.pallas_call(
        _inproj_kernel,
        grid=(b, n_t),
        in_specs=in_specs,
        out_specs=out_specs,
        out_shape=out_shape,
        compiler_params=pltpu.CompilerParams(dimension_semantics=("parallel", "parallel"),
                                             vmem_limit_bytes=VMEM_LIMIT),
        name="in_proj",
    )(x, shift1, scale1, norm1_w, wq, wk, wv, wx, wz, wdt, wg, gsum, qa, qb, ka, kb, dtb)


def _load_qt(qt_ref):
    return jnp.concatenate([qt_ref[r] for r in range(REP)], axis=1)


def _finish_attn(acc, o_ref, tq):
    o_t = acc[:HEAD_DIM] / acc[HEAD_DIM:HEAD_DIM + 1]
    o4 = jnp.concatenate([o_t[:, r * tq:(r + 1) * tq] for r in range(REP)], axis=0)
    o_ref[...] = o4.T.astype(o_ref.dtype)


def _attn_kernel(qt_ref, k_ref, vt_ref, o_ref, *, tq, tk):
    s_len = k_ref.shape[0]
    cols = REP * tq
    qt = _load_qt(qt_ref)

    def body(j, carry):
        m, acc = carry
        off = pl.multiple_of(j * tk, tk)
        sc = _dot(k_ref[pl.ds(off, tk), :], qt)
        m_new = jnp.maximum(m, jnp.max(sc, axis=0, keepdims=True))
        p = jnp.exp2(sc - m_new)
        alpha = jnp.exp2(m - m_new)
        acc = alpha * acc + _dot(vt_ref[:, pl.ds(off, tk)], p.astype(BF16))
        return m_new, acc

    m0 = jnp.full((1, cols), -jnp.inf, F32)
    acc0 = jnp.zeros((V_ROWS, cols), F32)
    _, acc = lax.fori_loop(0, s_len // tk, body, (m0, acc0))
    _finish_attn(acc, o_ref, tq)


def _attn_unshifted_kernel(qt_ref, k_ref, vt_ref, o_ref, *, tq, tk):
    s_len = k_ref.shape[0]
    nk = s_len // tk
    cols = REP * tq
    qt = _load_qt(qt_ref)

    def probs(j):
        off = pl.multiple_of(j * tk, tk)
        return jnp.exp2(_dot(k_ref[pl.ds(off, tk), :], qt)).astype(BF16)

    def pv(p, j):
        off = pl.multiple_of(j * tk, tk)
        return _dot(vt_ref[:, pl.ds(off, tk)], p)

    def body(j, carry):
        p, acc = carry
        p_next = probs(j + 1)
        return p_next, acc + pv(p, j)

    p, acc = lax.fori_loop(0, nk - 1, body, (probs(0), jnp.zeros((V_ROWS, cols), F32)), unroll=True)
    _finish_attn(acc + pv(p, nk - 1), o_ref, tq)


def _attention(qt, k, vt, tq, tk, body):
    b, _, _, s = qt.shape
    return pl.pallas_call(
        functools.partial(body, tq=tq, tk=tk),
        grid=(b, N_KV_HEADS, s // tq),
        in_specs=[
            pl.BlockSpec((None, REP, HEAD_DIM, tq), lambda bi, g, i: (bi, g, 0, i)),
            pl.BlockSpec((None, None, s, HEAD_DIM), lambda bi, g, i: (bi, g, 0, 0)),
            pl.BlockSpec((None, None, V_ROWS, s), lambda bi, g, i: (bi, g, 0, 0)),
        ],
        out_specs=pl.BlockSpec((None, tq, REP * HEAD_DIM), lambda bi, g, i: (bi, i, g)),
        out_shape=jax.ShapeDtypeStruct((b, s, N_Q_HEADS * HEAD_DIM), BF16),
        compiler_params=pltpu.CompilerParams(dimension_semantics=("parallel", "parallel", "arbitrary"),
                                             vmem_limit_bytes=VMEM_LIMIT),
        name="attn_online" if body is _attn_kernel else "attn_unshifted",
    )(qt, k, vt)


def _hi_mid(v):
    hi = v.astype(BF16)
    mid = (v - hi.astype(F32)).astype(BF16)
    return jnp.concatenate([hi, mid], axis=1)


def _ssd_scan_chunk(xs, bmat, cmat, dt, a_row, tmat, e2_ref, st_ref, emit, *, slot0, reverse):
    L = xs.shape[0]
    a = dt * a_row
    a_hi, a_mid, a_lo = _split3(a)
    cum = _dot(tmat, a_hi) + _dot(tmat, a_mid) + _dot(tmat, a_lo)
    total = cum[0:1, :] if reverse else cum[L - 1:L, :]
    src_t = (cum - jnp.log(dt)).T
    expand_lhs = _hi_mid(jnp.concatenate([jnp.exp(total - cum) * dt, jnp.exp(cum),
                                          jnp.broadcast_to(jnp.exp(total), (8, LANES))], axis=0))

    li = lax.broadcasted_iota(jnp.int32, (L, L), 0)
    si = lax.broadcasted_iota(jnp.int32, (L, L), 1)
    keep = (si >= li) if reverse else (li >= si)
    lane = lax.broadcasted_iota(jnp.int32, (L, LANES), 1)
    first_head = lane < SSD_P

    for g in range(SSD_G):
        gs = slice(g * SSD_N, (g + 1) * SSD_N)
        cols = slice(g * HEADS_PER_GROUP * SSD_P, (g + 1) * HEADS_PER_GROUP * SSD_P)
        e2_g = e2_ref[:, cols]
        bg = bmat[:, gs]
        cg_bf = cmat[:, gs].astype(BF16)
        cb = _dot_nt(cg_bf, bg.astype(BF16))
        st_g = st_ref[:, cols]
        expanded = _dot(expand_lhs, e2_g)
        y_off = _dot(cg_bf, st_g.astype(BF16)) * expanded[L:2 * L, :]
        xw = (xs[:, cols] * expanded[:L, :]).astype(BF16)
        st_ref[:, cols] = expanded[2 * L:2 * L + 1, :] * st_g + _dot(bg.T.astype(BF16), xw)
        for pr in range(HEADS_PER_GROUP // 2):
            h0 = g * HEADS_PER_GROUP + 2 * pr
            ms = []
            for hh in (h0, h0 + 1):
                j = slot0 + hh
                arg = cum[:, j:j + 1] - src_t[j:j + 1, :]
                ms.append((cb * jnp.exp(jnp.where(keep, arg, NEG_BIG))).astype(BF16))
            lhs = jnp.concatenate(ms, axis=1)
            pc = slice(h0 * SSD_P, h0 * SSD_P + LANES)
            xp = xs[:, pc]
            rhs = jnp.concatenate([jnp.where(first_head, xp, 0.0).astype(BF16),
                                   jnp.where(first_head, 0.0, xp).astype(BF16)], axis=0)
            emit(pc, _dot(lhs, rhs) + y_off[:, pr * LANES:(pr + 1) * LANES])


def _ssd_fwd_kernel(x_ref, xprev_ref, xnext_ref, dt_ref, shift_ref, cw_ref, cb_ref, a_ref, dskip_ref, tmat_ref,
                    e2_ref, xc_out, y_out, st_ref):
    c = pl.program_id(1)
    nc = pl.num_programs(1)
    L = CHUNK

    @pl.when(c == 0)
    def _():
        st_ref[...] = jnp.zeros_like(st_ref)

    zero_halo = jnp.zeros((CONV_HALO, CONV_DIM), BF16)
    before = jnp.where(c > 0, xprev_ref[...], zero_halo)
    after = jnp.where(c < nc - 1, xnext_ref[...], zero_halo)
    shift = shift_ref[...]
    mid_tap = (D_CONV - 1) // 2
    taps = [k for k in range(D_CONV) if k != mid_tap]
    a_row = a_ref[...]
    tmat = tmat_ref[...]

    for sub in range(CHUNKS_PER_STEP):
        r0 = sub * L
        rows = slice(r0, r0 + L)
        x_mid = x_ref[rows, :]
        prev = before if sub == 0 else x_ref[r0 - CONV_HALO:r0, :]
        nxt = after if sub == CHUNKS_PER_STEP - 1 else x_ref[r0 + L:r0 + L + CONV_HALO, :]
        ext = jnp.concatenate([prev, x_mid, nxt, jnp.zeros((SHIFT_K - L - 2 * CONV_HALO, CONV_DIM), BF16)], axis=0)
        acts = []
        for blk in range(CONV_DIM // 512):
            sl = slice(blk * 512, (blk + 1) * 512)
            shifted = _dot(shift, ext[:, sl])
            acc = cb_ref[:, sl] + cw_ref[mid_tap:mid_tap + 1, sl] * x_mid[:, sl].astype(F32)
            for i, k in enumerate(taps):
                acc = acc + cw_ref[k:k + 1, sl] * shifted[i * L:(i + 1) * L, :]
            act = acc * jax.nn.sigmoid(acc)
            xc_out[rows, sl] = act.astype(BF16)
            acts.append(act)
        xs = jnp.concatenate(acts[:D_INNER // 512], axis=1)
        bmat = acts[D_INNER // 512]
        cmat = acts[D_INNER // 512 + 1]

        def emit(pc, y, rows=rows, xs=xs):
            y_out[rows, pc] = y + dskip_ref[:, pc] * xs[:, pc]

        _ssd_scan_chunk(xs, bmat, cmat, dt_ref[rows, :], a_row, tmat, e2_ref, st_ref, emit, slot0=0, reverse=False)


def _ssd_bwd_kernel(xc_ref, dt_ref, yf_ref, zs_ref, a_ref, nw_ref, tmat_ref, e2_ref, o_ref, st_ref, y_ref):
    c = pl.program_id(1)
    L = CHUNK

    @pl.when(c == 0)
    def _():
        st_ref[...] = jnp.zeros_like(st_ref)

    a_row = a_ref[...]
    tmat = tmat_ref[...]
    for sub in reversed(range(CHUNKS_PER_STEP)):
        rows = slice(sub * L, (sub + 1) * L)
        xc = xc_ref[rows, :].astype(F32)
        xs = xc[:, :D_INNER]
        bmat = xc[:, D_INNER:D_INNER + SSD_G * SSD_N]
        cmat = xc[:, D_INNER + SSD_G * SSD_N:]
        sq = []

        def emit(pc, y, rows=rows, sq=sq):
            gated = (y + yf_ref[rows, pc]) * zs_ref[rows, pc].astype(F32)
            y_ref[rows, pc] = gated
            sq.append(gated * gated)

        _ssd_scan_chunk(xs, bmat, cmat, dt_ref[rows, :], a_row, tmat, e2_ref, st_ref, emit, slot0=SSD_H,
                        reverse=True)
        ms = jnp.sum(functools.reduce(lambda u, v: u + v, sq), axis=-1, keepdims=True) * (1.0 / D_INNER)
        o_ref[rows, :] = (y_ref[rows, :] * lax.rsqrt(ms + EPS) * nw_ref[...]).astype(o_ref.dtype)


def _conv_shift_matrix():
    pad = (D_CONV - 1) // 2
    taps = [k for k in range(D_CONV) if k != pad]
    m = np.zeros((len(taps) * CHUNK, SHIFT_K), np.float32)
    for i, k in enumerate(taps):
        for l in range(CHUNK):
            m[i * CHUNK + l, CONV_HALO + l + k - pad] = 1.0
    return jnp.asarray(m, BF16)


def _ssd(xbc, dt, zs, conv_w, conv_b, a_row, dskip, ssd_norm_w, tri_lo, tri_up, e2_f, e2_b):
    b, s, _ = xbc.shape
    shift = _conv_shift_matrix()
    rows = CHUNK * CHUNKS_PER_STEP
    assert s % rows == 0
    nb = s // rows
    hb = rows // CONV_HALO
    n_halo = s // CONV_HALO
    row = lambda bi, c: (bi, c, 0)
    params = pltpu.CompilerParams(dimension_semantics=("parallel", "arbitrary"), vmem_limit_bytes=VMEM_LIMIT)
    xc, yf = pl.pallas_call(
        _ssd_fwd_kernel,
        grid=(b, nb),
        in_specs=[
            pl.BlockSpec((None, rows, CONV_DIM), row),
            pl.BlockSpec((None, CONV_HALO, CONV_DIM), lambda bi, c: (bi, jnp.maximum(c * hb - 1, 0), 0)),
            pl.BlockSpec((None, CONV_HALO, CONV_DIM), lambda bi, c: (bi, jnp.minimum((c + 1) * hb, n_halo - 1), 0)),
            pl.BlockSpec((None, rows, LANES), row),
            _const_spec(shift.shape), _const_spec(conv_w.shape), _const_spec(conv_b.shape),
            _const_spec(a_row.shape), _const_spec(dskip.shape), _const_spec(tri_lo.shape), _const_spec(e2_f.shape),
        ],
        out_specs=[pl.BlockSpec((None, rows, CONV_DIM), row), pl.BlockSpec((None, rows, D_INNER), row)],
        out_shape=[jax.ShapeDtypeStruct((b, s, CONV_DIM), BF16), jax.ShapeDtypeStruct((b, s, D_INNER), F32)],
        scratch_shapes=[pltpu.VMEM((SSD_N, D_INNER), F32)],
        compiler_params=params,
        name="ssd_fwd",
    )(xbc, xbc, xbc, dt, shift, conv_w, conv_b, a_row, dskip, tri_lo, e2_f)

    rev = lambda bi, c: (bi, nb - 1 - c, 0)
    return pl.pallas_call(
        _ssd_bwd_kernel,
        grid=(b, nb),
        in_specs=[
            pl.BlockSpec((None, rows, CONV_DIM), rev),
            pl.BlockSpec((None, rows, LANES), rev),
            pl.BlockSpec((None, rows, D_INNER), rev),
            pl.BlockSpec((None, rows, D_INNER), rev),
            _const_spec(a_row.shape), _const_spec(ssd_norm_w.shape), _const_spec(tri_up.shape),
            _const_spec(e2_b.shape),
        ],
        out_specs=pl.BlockSpec((None, rows, D_INNER), rev),
        out_shape=jax.ShapeDtypeStruct((b, s, D_INNER), BF16),
        scratch_shapes=[pltpu.VMEM((SSD_N, D_INNER), F32), pltpu.VMEM((rows, D_INNER), F32)],
        compiler_params=params,
        name="ssd_bwd",
    )(xc, dt, yf, zs, a_row, ssd_norm_w, tri_up, e2_b)


def _merge_kernel(attn_ref, ssd_ref, g_ref, x_ref, gate1_ref, shift2_ref, scale2_ref, nw2_ref,
                  wao_ref, wso_ref, wo_ref, x1_out, h2_out):
    a = _dot(attn_ref[...], wao_ref[...])
    sm = _dot(ssd_ref[...], wso_ref[...])
    g = g_ref[...].astype(F32)
    merged = (g[:, :D_MODEL] * a + g[:, D_MODEL:] * sm).astype(BF16)
    x1 = x_ref[...] + gate1_ref[...] * _dot(merged, wo_ref[...])
    x1_out[...] = x1
    ms = jnp.mean(x1 * x1, axis=-1, keepdims=True)
    xn = x1 * lax.rsqrt(ms + EPS)
    h2_out[...] = (xn * (nw2_ref[...] * (1.0 + scale2_ref[...])) + shift2_ref[...]).astype(BF16)


def _merge(attn, ssd, gates, x, gate1, shift2, scale2, norm2_w, wao, wso, wo, tm):
    b, s, d = x.shape
    row = lambda bi, i: (bi, i, 0)
    per_b = pl.BlockSpec((None, 1, d), lambda bi, i: (bi, 0, 0))
    return pl.pallas_call(
        _merge_kernel,
        grid=(b, s // tm),
        in_specs=[
            pl.BlockSpec((None, tm, attn.shape[-1]), row), pl.BlockSpec((None, tm, D_INNER), row),
            pl.BlockSpec((None, tm, 2 * d), row), pl.BlockSpec((None, tm, d), row),
            per_b, per_b, per_b, _const_spec((1, d)),
            _const_spec(wao.shape), _const_spec(wso.shape), _const_spec(wo.shape),
        ],
        out_specs=[pl.BlockSpec((None, tm, d), row), pl.BlockSpec((None, tm, d), row)],
        out_shape=[jax.ShapeDtypeStruct((b, s, d), F32), jax.ShapeDtypeStruct((b, s, d), BF16)],
        compiler_params=pltpu.CompilerParams(dimension_semantics=("parallel", "parallel"),
                                             vmem_limit_bytes=VMEM_LIMIT),
        name="merge",
    )(attn, ssd, gates, x, gate1, shift2, scale2, norm2_w, wao, wso, wo)


def _mlp_kernel(h2_ref, x1_ref, gate2_ref, w1_ref, w2_ref, o_ref):
    h2 = h2_ref[...]
    acc = jnp.zeros(x1_ref.shape, F32)
    for blk in range(D_FF // 1024):
        sl = slice(blk * 1024, (blk + 1) * 1024)
        u = jnp.maximum(_dot(h2, w1_ref[:, sl]), 0.0)
        acc = acc + _dot((u * u).astype(BF16), w2_ref[sl, :])
    o_ref[...] = x1_ref[...] + gate2_ref[...] * acc


def _mlp(h2, x1, gate2, w1, w2, tm):
    b, s, d = x1.shape
    row = lambda bi, i: (bi, i, 0)
    return pl.pallas_call(
        _mlp_kernel,
        grid=(b, s // tm),
        in_specs=[
            pl.BlockSpec((None, tm, d), row), pl.BlockSpec((None, tm, d), row),
            pl.BlockSpec((None, 1, d), lambda bi, i: (bi, 0, 0)),
            _const_spec(w1.shape), _const_spec(w2.shape),
        ],
        out_specs=pl.BlockSpec((None, tm, d), row),
        out_shape=jax.ShapeDtypeStruct((b, s, d), F32),
        compiler_params=pltpu.CompilerParams(dimension_semantics=("parallel", "parallel"),
                                             vmem_limit_bytes=VMEM_LIMIT),
        name="mlp",
    )(h2, x1, gate2, w1, w2)


def _swap_halves(w):
    return w.reshape(2, 2, 16)[:, ::-1, :].reshape(HEAD_DIM)


def _rope_tables(s, norm_w, scale):
    pos = jnp.arange(s, dtype=jnp.int32)
    pos_row = (pos // GRID_W).astype(F32)
    pos_col = (pos % GRID_W).astype(F32)
    axis_dim = HEAD_DIM // 2
    inv_freq = ROPE_THETA ** (-jnp.arange(0, axis_dim, 2, dtype=F32) / axis_dim)
    ang_r = pos_row[:, None] * inv_freq[None, :]
    ang_c = pos_col[:, None] * inv_freq[None, :]
    cos = jnp.concatenate([jnp.cos(ang_r)] * 2 + [jnp.cos(ang_c)] * 2, axis=-1)
    sin_r, sin_c = jnp.sin(ang_r), jnp.sin(ang_c)
    sin = jnp.concatenate([-sin_r, sin_r, -sin_c, sin_c], axis=-1)
    ta = cos * norm_w[None, :] * scale
    tb = sin * _swap_halves(norm_w)[None, :] * scale
    return jnp.tile(ta, (1, 2)), jnp.tile(tb, (1, 2))


def _head_expansion(slot0):
    e = np.zeros((LANES, D_INNER), np.float32)
    for h in range(SSD_H):
        e[slot0 + h, h * SSD_P:(h + 1) * SSD_P] = 1.0
    return jnp.asarray(np.concatenate([e, e], axis=0), BF16)


def _pick_tile(s, pref):
    t = min(pref, s)
    assert s % t == 0
    return t


def kernel(x, c, w_ada, b_ada, norm1_w, norm2_w, w_in, q_norm_w, k_norm_w, conv_w, conv_b, A_log, dt_bias,
           ssd_D, ssd_norm_w, w_attn_out, w_ssd_out, w_o, w_mlp1, w_mlp2):
    b, s, d = x.shape
    depth = w_ada.shape[0]
    assert d == D_MODEL and s % (CHUNK * CHUNKS_PER_STEP) == 0 and s % GRID_W == 0

    qdim = N_Q_HEADS * HEAD_DIM
    kvdim = N_KV_HEADS * HEAD_DIM
    o_k, o_v, o_x = qdim, qdim + kvdim, qdim + 2 * kvdim
    o_z = o_x + CONV_DIM
    o_dt = o_z + D_INNER
    o_g = o_dt + 2 * SSD_H

    gsum = jnp.asarray(np.kron(np.eye(4, dtype=np.float32), np.full((HEAD_DIM, HEAD_DIM), 1.0 / HEAD_DIM,
                                                                       np.float32)), BF16)
    tri = np.tril(np.ones((CHUNK, CHUNK), np.float32))
    tri_lo = jnp.asarray(tri, BF16)
    tri_up = jnp.asarray(tri.T, BF16)
    e2_f = _head_expansion(0)
    e2_b = _head_expansion(SSD_H)
    ts = _pick_tile(s, 512)
    tq = _pick_tile(s, 256)
    tk = _pick_tile(s, 256)

    for l in range(depth):
        mod = _ada(c, w_ada[l], b_ada[l])
        shift1, scale1, gate1, shift2, scale2, gate2 = [m[:, None, :] for m in jnp.split(mod, 6, axis=-1)]

        w = w_in[l]
        wq = w[:, :o_k].astype(BF16)
        wk = w[:, o_k:o_v].astype(BF16)
        wv = w[:, o_v:o_x].T.astype(BF16)
        wx = w[:, o_x:o_z].astype(BF16)
        wz = w[:, o_z:o_dt].astype(BF16)
        wdt = jnp.pad(w[:, o_dt:o_g], ((0, 0), (0, LANES - 2 * SSD_H))).astype(BF16)
        wg = w[:, o_g:].astype(BF16)
        dtb = jnp.pad(dt_bias[l].reshape(1, 2 * SSD_H), ((0, 0), (0, LANES - 2 * SSD_H)))
        q_scale = math.log2(math.e) * HEAD_DIM ** -0.5
        qa, qb = _rope_tables(s, q_norm_w[l], q_scale)
        ka, kb = _rope_tables(s, k_norm_w[l], 1.0)

        qt, k, vt, xbc, zs, dt, gates = _inproj(x, shift1, scale1, norm1_w[l].reshape(1, d), wq, wk, wv, wx, wz,
                                                wdt, wg, gsum, qa, qb, ka, kb, dtb, ts)
        score_bound = (q_scale * HEAD_DIM * jnp.max(jnp.abs(q_norm_w[l])) * jnp.max(jnp.abs(k_norm_w[l])))
        attn = lax.cond(score_bound <= MAX_UNSHIFTED_SCORE,
                        lambda *a: _attention(*a, tq, tk, _attn_unshifted_kernel),
                        lambda *a: _attention(*a, tq, tk, _attn_kernel), qt, k, vt)

        a_row = jnp.pad(-jnp.exp(A_log[l].astype(F32)).reshape(1, 2 * SSD_H), ((0, 0), (0, LANES - 2 * SSD_H)))
        dskip = jnp.repeat(ssd_D[l].astype(F32), SSD_P).reshape(1, D_INNER)
        ssd = _ssd(xbc, dt, zs, conv_w[l], conv_b[l].reshape(1, CONV_DIM), a_row, dskip,
                   ssd_norm_w[l].reshape(1, D_INNER), tri_lo, tri_up, e2_f, e2_b)

        x1, h2 = _merge(attn, ssd, gates, x, gate1, shift2, scale2, norm2_w[l].reshape(1, d),
                        w_attn_out[l].astype(BF16), w_ssd_out[l].astype(BF16), w_o[l].astype(BF16), ts)
        x = _mlp(h2, x1, gate2, w_mlp1[l].astype(BF16), w_mlp2[l].astype(BF16), ts)
    return x
```

```python
import functools
import math

import jax
import jax.numpy as jnp
import numpy as np
from jax import lax
from jax.experimental import pallas as pl
from jax.experimental.pallas import tpu as pltpu

F32 = jnp.float32
BF16 = jnp.bfloat16

D_MODEL = 1024
GRID_W = 64
N_Q_HEADS = 16
N_KV_HEADS = 4
REP = N_Q_HEADS // N_KV_HEADS
HEAD_DIM = 64
ROPE_THETA = 10000.0
D_INNER = 2048
SSD_P = 64
SSD_H = D_INNER // SSD_P
SSD_G = 4
SSD_N = 128
HEADS_PER_GROUP = SSD_H // SSD_G
D_CONV = 5
CHUNK = 128
CONV_DIM = D_INNER + 2 * SSD_G * SSD_N
D_FF = 4 * D_MODEL
EPS = 1e-6

LANES = 128
V_ROWS = 80
CONV_HALO = 16
CHUNKS_PER_STEP = 4
SHIFT_K = 256
NEG_BIG = -1e30
MAX_UNSHIFTED_SCORE = 60.0
VMEM_LIMIT = 56 * 1024 * 1024


def _dot(a, b):
    return jnp.dot(a, b, preferred_element_type=F32)


def _dot_nt(a, b):
    return lax.dot_general(a, b, (((1,), (1,)), ((), ())), preferred_element_type=F32)


def _split3(v):
    hi = v.astype(BF16)
    r = v - hi.astype(F32)
    mid = r.astype(BF16)
    lo = (r - mid.astype(F32)).astype(BF16)
    return hi, mid, lo


def _const_spec(shape):
    return pl.BlockSpec(shape, lambda *_: (0,) * len(shape))


def _ada_kernel(c_ref, w_ref, b_ref, o_ref):
    c = c_ref[...]
    sc = c * jax.nn.sigmoid(c)
    a_hi, a_mid, a_lo = _split3(sc)
    w_hi, w_mid, w_lo = _split3(w_ref[...])
    acc = _dot(a_hi, w_hi)
    acc += _dot(a_hi, w_mid) + _dot(a_mid, w_hi)
    acc += _dot(a_hi, w_lo) + _dot(a_mid, w_mid) + _dot(a_lo, w_hi)
    o_ref[...] = acc + b_ref[...]


def _ada(c, w_ada, b_ada):
    b, d = c.shape
    n = w_ada.shape[1]
    tn = 1536
    return pl.pallas_call(
        _ada_kernel,
        grid=(n // tn,),
        in_specs=[
            pl.BlockSpec((b, d), lambda j: (0, 0)),
            pl.BlockSpec((d, tn), lambda j: (0, j)),
            pl.BlockSpec((1, tn), lambda j: (0, j)),
        ],
        out_specs=pl.BlockSpec((b, tn), lambda j: (0, j)),
        out_shape=jax.ShapeDtypeStruct((b, n), F32),
        compiler_params=pltpu.CompilerParams(dimension_semantics=("arbitrary",), vmem_limit_bytes=VMEM_LIMIT),
        name="ada",
    )(c, w_ada, b_ada.reshape(1, n))


def _norm_rope_block(t, ss, ta, tb, swap_lo):
    n = t * lax.rsqrt(ss + EPS)
    fwd = pltpu.roll(n, LANES - 16, axis=1)
    bwd = pltpu.roll(n, 16, axis=1)
    sw = jnp.where(swap_lo, fwd, bwd)
    return n * ta + sw * tb


def _inproj_kernel(x_ref, shift_ref, scale_ref, nw_ref, wq_ref, wk_ref, wv_ref, wx_ref, wz_ref, wdt_ref, wg_ref,
                   gsum_ref, qa_ref, qb_ref, ka_ref, kb_ref, dtb_ref,
                   q_out, k_out, vt_out, xbc_out, z_out, dt_out, g_out):
    x = x_ref[...]
    ts = x.shape[0]
    ms = jnp.mean(x * x, axis=-1, keepdims=True)
    xn = x * lax.rsqrt(ms + EPS)
    h = (xn * (nw_ref[...] * (1.0 + scale_ref[...])) + shift_ref[...]).astype(BF16)

    lane = lax.broadcasted_iota(jnp.int32, (ts, LANES), 1)
    swap_lo = (lane & 31) < 16
    gsum = gsum_ref[...]

    qf = _dot(h, wq_ref[...])
    qa = qa_ref[...]
    qb = qb_ref[...]
    for blk in range(N_Q_HEADS * HEAD_DIM // 256):
        t = qf[:, blk * 256:(blk + 1) * 256]
        ss = _dot((t * t).astype(BF16), gsum)
        for half in range(2):
            sl = slice(half * LANES, (half + 1) * LANES)
            rt = _norm_rope_block(t[:, sl], ss[:, sl], qa, qb, swap_lo).T.astype(BF16)
            hd = blk * 4 + half * 2
            q_out[hd] = rt[:HEAD_DIM, :]
            q_out[hd + 1] = rt[HEAD_DIM:, :]

    kf = _dot(h, wk_ref[...])
    ssk = _dot((kf * kf).astype(BF16), gsum)
    ka = ka_ref[...]
    kb = kb_ref[...]
    for half in range(2):
        sl = slice(half * LANES, (half + 1) * LANES)
        r = _norm_rope_block(kf[:, sl], ssk[:, sl], ka, kb, swap_lo).astype(BF16)
        k_out[2 * half] = r[:, :HEAD_DIM]
        k_out[2 * half + 1] = r[:, HEAD_DIM:]

    vt = _dot_nt(wv_ref[...], h).astype(BF16)
    ones_rows = (lax.broadcasted_iota(jnp.int32, (V_ROWS - HEAD_DIM, ts), 0) == 0).astype(BF16)
    for g in range(N_KV_HEADS):
        vt_out[g, :HEAD_DIM, :] = vt[g * HEAD_DIM:(g + 1) * HEAD_DIM, :]
        vt_out[g, HEAD_DIM:, :] = ones_rows

    for blk in range(CONV_DIM // 512):
        sl = slice(blk * 512, (blk + 1) * 512)
        xbc_out[:, sl] = _dot(h, wx_ref[:, sl]).astype(BF16)

    for blk in range(D_INNER // 512):
        sl = slice(blk * 512, (blk + 1) * 512)
        zf = _dot(h, wz_ref[:, sl])
        z_out[:, sl] = (zf * jax.nn.sigmoid(zf)).astype(BF16)

    dtf = _dot(h, wdt_ref[...]) + dtb_ref[...]
    dt_out[...] = jnp.maximum(dtf, 0.0) + jnp.log1p(jnp.exp(-jnp.abs(dtf)))

    for blk in range(2 * D_MODEL // 512):
        sl = slice(blk * 512, (blk + 1) * 512)
        g_out[:, sl] = jax.nn.sigmoid(_dot(h, wg_ref[:, sl])).astype(BF16)


def _inproj(x, shift1, scale1, norm1_w, wq, wk, wv, wx, wz, wdt, wg, gsum, qa, qb, ka, kb, dtb, ts):
    b, s, d = x.shape
    n_t = s // ts
    row = lambda bi, i: (bi, i, 0)
    per_b = pl.BlockSpec((None, 1, d), lambda bi, i: (bi, 0, 0))
    tab = pl.BlockSpec((ts, LANES), lambda bi, i: (i, 0))
    in_specs = [
        pl.BlockSpec((None, ts, d), row), per_b, per_b, _const_spec((1, d)),
        _const_spec(wq.shape), _const_spec(wk.shape), _const_spec(wv.shape), _const_spec(wx.shape),
        _const_spec(wz.shape), _const_spec(wdt.shape), _const_spec(wg.shape), _const_spec(gsum.shape),
        tab, tab, tab, tab, _const_spec(dtb.shape),
    ]
    out_specs = [
        pl.BlockSpec((None, N_Q_HEADS, HEAD_DIM, ts), lambda bi, i: (bi, 0, 0, i)),
        pl.BlockSpec((None, N_KV_HEADS, ts, HEAD_DIM), lambda bi, i: (bi, 0, i, 0)),
        pl.BlockSpec((None, N_KV_HEADS, V_ROWS, ts), lambda bi, i: (bi, 0, 0, i)),
        pl.BlockSpec((None, ts, CONV_DIM), row),
        pl.BlockSpec((None, ts, D_INNER), row),
        pl.BlockSpec((None, ts, LANES), row),
        pl.BlockSpec((None, ts, 2 * D_MODEL), row),
    ]
    out_shape = [
        jax.ShapeDtypeStruct((b, N_Q_HEADS, HEAD_DIM, s), BF16),
        jax.ShapeDtypeStruct((b, N_KV_HEADS, s, HEAD_DIM), BF16),
        jax.ShapeDtypeStruct((b, N_KV_HEADS, V_ROWS, s), BF16),
        jax.ShapeDtypeStruct((b, s, CONV_DIM), BF16),
        jax.ShapeDtypeStruct((b, s, D_INNER), BF16),
        jax.ShapeDtypeStruct((b, s, LANES), F32),
        jax.ShapeDtypeStruct((b, s, 2 * D_MODEL), BF16),
    ]
    return pl.pallas_call(
        _inproj_kernel,
        grid=(b, n_t),
        in_specs=in_specs,
        out_specs=out_specs,
        out_shape=out_shape,
        compiler_params=pltpu.CompilerParams(dimension_semantics=("parallel", "parallel"),
                                             vmem_limit_bytes=VMEM_LIMIT),
        name="in_proj",
    )(x, shift1, scale1, norm1_w, wq, wk, wv, wx, wz, wdt, wg, gsum, qa, qb, ka, kb, dtb)


def _load_qt(qt_ref):
    return jnp.concatenate([qt_ref[r] for r in range(REP)], axis=1)


def _finish_attn(acc, o_ref, tq):
    o_t = acc[:HEAD_DIM] / acc[HEAD_DIM:HEAD_DIM + 1]
    o4 = jnp.concatenate([o_t[:, r * tq:(r + 1) * tq] for r in range(REP)], axis=0)
    o_ref[...] = o4.T.astype(o_ref.dtype)


def _attn_kernel(qt_ref, k_ref, vt_ref, o_ref, *, tq, tk):
    s_len = k_ref.shape[0]
    cols = REP * tq
    qt = _load_qt(qt_ref)

    def body(j, carry):
        m, acc = carry
        off = pl.multiple_of(j * tk, tk)
        sc = _dot(k_ref[pl.ds(off, tk), :], qt)
        m_new = jnp.maximum(m, jnp.max(sc, axis=0, keepdims=True))
        p = jnp.exp2(sc - m_new)
        alpha = jnp.exp2(m - m_new)
        acc = alpha * acc + _dot(vt_ref[:, pl.ds(off, tk)], p.astype(BF16))
        return m_new, acc

    m0 = jnp.full((1, cols), -jnp.inf, F32)
    acc0 = jnp.zeros((V_ROWS, cols), F32)
    _, acc = lax.fori_loop(0, s_len // tk, body, (m0, acc0))
    _finish_attn(acc, o_ref, tq)


def _attn_unshifted_kernel(qt_ref, k_ref, vt_ref, o_ref, *, tq, tk):
    s_len = k_ref.shape[0]
    nk = s_len // tk
    cols = REP * tq
    qt = _load_qt(qt_ref)

    def probs(j):
        off = pl.multiple_of(j * tk, tk)
        return jnp.exp2(_dot(k_ref[pl.ds(off, tk), :], qt)).astype(BF16)

    def pv(p, j):
        off = pl.multiple_of(j * tk, tk)
        return _dot(vt_ref[:, pl.ds(off, tk)], p)

    def body(j, carry):
        p, acc = carry
        p_next = probs(j + 1)
        return p_next, acc + pv(p, j)

    p, acc = lax.fori_loop(0, nk - 1, body, (probs(0), jnp.zeros((V_ROWS, cols), F32)), unroll=True)
    _finish_attn(acc + pv(p, nk - 1), o_ref, tq)


def _attention(qt, k, vt, tq, tk, body):
    b, _, _, s = qt.shape
    return pl.pallas_call(
        functools.partial(body, tq=tq, tk=tk),
        grid=(b, N_KV_HEADS, s // tq),
        in_specs=[
            pl.BlockSpec((None, REP, HEAD_DIM, tq), lambda bi, g, i: (bi, g, 0, i)),
            pl.BlockSpec((None, None, s, HEAD_DIM), lambda bi, g, i: (bi, g, 0, 0)),
            pl.BlockSpec((None, None, V_ROWS, s), lambda bi, g, i: (bi, g, 0, 0)),
        ],
        out_specs=pl.BlockSpec((None, tq, REP * HEAD_DIM), lambda bi, g, i: (bi, i, g)),
        out_shape=jax.ShapeDtypeStruct((b, s, N_Q_HEADS * HEAD_DIM), BF16),
        compiler_params=pltpu.CompilerParams(dimension_semantics=("parallel", "parallel", "arbitrary"),
                                             vmem_limit_bytes=VMEM_LIMIT),
        name="attn_online" if body is _attn_kernel else "attn_unshifted",
    )(qt, k, vt)


def _hi_mid(v):
    hi = v.astype(BF16)
    mid = (v - hi.astype(F32)).astype(BF16)
    return jnp.concatenate([hi, mid], axis=1)


def _ssd_scan_chunk(xs, bmat, cmat, dt, a_row, tmat, e2_ref, st_ref, emit, *, slot0, reverse):
    L = xs.shape[0]
    a = dt * a_row
    a_hi, a_mid, a_lo = _split3(a)
    cum = _dot(tmat, a_hi) + _dot(tmat, a_mid) + _dot(tmat, a_lo)
    total = cum[0:1, :] if reverse else cum[L - 1:L, :]
    src_t = (cum - jnp.log(dt)).T
    expand_lhs = _hi_mid(jnp.concatenate([jnp.exp(total - cum) * dt, jnp.exp(cum),
                                          jnp.broadcast_to(jnp.exp(total), (8, LANES))], axis=0))

    li = lax.broadcasted_iota(jnp.int32, (L, L), 0)
    si = lax.broadcasted_iota(jnp.int32, (L, L), 1)
    keep = (si >= li) if reverse else (li >= si)
    lane = lax.broadcasted_iota(jnp.int32, (L, LANES), 1)
    first_head = lane < SSD_P

    for g in range(SSD_G):
        gs = slice(g * SSD_N, (g + 1) * SSD_N)
        cols = slice(g * HEADS_PER_GROUP * SSD_P, (g + 1) * HEADS_PER_GROUP * SSD_P)
        e2_g = e2_ref[:, cols]
        bg = bmat[:, gs]
        cg_bf = cmat[:, gs].astype(BF16)
        cb = _dot_nt(cg_bf, bg.astype(BF16))
        st_g = st_ref[:, cols]
        expanded = _dot(expand_lhs, e2_g)
        y_off = _dot(cg_bf, st_g.astype(BF16)) * expanded[L:2 * L, :]
        xw = (xs[:, cols] * expanded[:L, :]).astype(BF16)
        st_ref[:, cols] = expanded[2 * L:2 * L + 1, :] * st_g + _dot(bg.T.astype(BF16), xw)
        for pr in range(HEADS_PER_GROUP // 2):
            h0 = g * HEADS_PER_GROUP + 2 * pr
            ms = []
            for hh in (h0, h0 + 1):
                j = slot0 + hh
                arg = cum[:, j:j + 1] - src_t[j:j + 1, :]
                ms.append((cb * jnp.exp(jnp.where(keep, arg, NEG_BIG))).astype(BF16))
            lhs = jnp.concatenate(ms, axis=1)
            pc = slice(h0 * SSD_P, h0 * SSD_P + LANES)
            xp = xs[:, pc]
            rhs = jnp.concatenate([jnp.where(first_head, xp, 0.0).astype(BF16),
                                   jnp.where(first_head, 0.0, xp).astype(BF16)], axis=0)
            emit(pc, _dot(lhs, rhs) + y_off[:, pr * LANES:(pr + 1) * LANES])


def _ssd_fwd_kernel(x_ref, xprev_ref, xnext_ref, dt_ref, shift_ref, cw_ref, cb_ref, a_ref, dskip_ref, tmat_ref,
                    e2_ref, xc_out, y_out, st_ref):
    c = pl.program_id(1)
    nc = pl.num_programs(1)
    L = CHUNK

    @pl.when(c == 0)
    def _():
        st_ref[...] = jnp.zeros_like(st_ref)

    zero_halo = jnp.zeros((CONV_HALO, CONV_DIM), BF16)
    before = jnp.where(c > 0, xprev_ref[...], zero_halo)
    after = jnp.where(c < nc - 1, xnext_ref[...], zero_halo)
    shift = shift_ref[...]
    mid_tap = (D_CONV - 1) // 2
    taps = [k for k in range(D_CONV) if k != mid_tap]
    a_row = a_ref[...]
    tmat = tmat_ref[...]

    for sub in range(CHUNKS_PER_STEP):
        r0 = sub * L
        rows = slice(r0, r0 + L)
        x_mid = x_ref[rows, :]
        prev = before if sub == 0 else x_ref[r0 - CONV_HALO:r0, :]
        nxt = after if sub == CHUNKS_PER_STEP - 1 else x_ref[r0 + L:r0 + L + CONV_HALO, :]
        ext = jnp.concatenate([prev, x_mid, nxt, jnp.zeros((SHIFT_K - L - 2 * CONV_HALO, CONV_DIM), BF16)], axis=0)
        acts = []
        for blk in range(CONV_DIM // 512):
            sl = slice(blk * 512, (blk + 1) * 512)
            shifted = _dot(shift, ext[:, sl])
            acc = cb_ref[:, sl] + cw_ref[mid_tap:mid_tap + 1, sl] * x_mid[:, sl].astype(F32)
            for i, k in enumerate(taps):
                acc = acc + cw_ref[k:k + 1, sl] * shifted[i * L:(i + 1) * L, :]
            act = acc * jax.nn.sigmoid(acc)
            xc_out[rows, sl] = act.astype(BF16)
            acts.append(act)
        xs = jnp.concatenate(acts[:D_INNER // 512], axis=1)
        bmat = acts[D_INNER // 512]
        cmat = acts[D_INNER // 512 + 1]

        def emit(pc, y, rows=rows, xs=xs):
            y_out[rows, pc] = y + dskip_ref[:, pc] * xs[:, pc]

        _ssd_scan_chunk(xs, bmat, cmat, dt_ref[rows, :], a_row, tmat, e2_ref, st_ref, emit, slot0=0, reverse=False)


def _ssd_bwd_kernel(xc_ref, dt_ref, yf_ref, zs_ref, a_ref, nw_ref, tmat_ref, e2_ref, o_ref, st_ref, y_ref):
    c = pl.program_id(1)
    L = CHUNK

    @pl.when(c == 0)
    def _():
        st_ref[...] = jnp.zeros_like(st_ref)

    a_row = a_ref[...]
    tmat = tmat_ref[...]
    for sub in reversed(range(CHUNKS_PER_STEP)):
        rows = slice(sub * L, (sub + 1) * L)
        xc = xc_ref[rows, :].astype(F32)
        xs = xc[:, :D_INNER]
        bmat = xc[:, D_INNER:D_INNER + SSD_G * SSD_N]
        cmat = xc[:, D_INNER + SSD_G * SSD_N:]
        sq = []

        def emit(pc, y, rows=rows, sq=sq):
            gated = (y + yf_ref[rows, pc]) * zs_ref[rows, pc].astype(F32)
            y_ref[rows, pc] = gated
            sq.append(gated * gated)

        _ssd_scan_chunk(xs, bmat, cmat, dt_ref[rows, :], a_row, tmat, e2_ref, st_ref, emit, slot0=SSD_H,
                        reverse=True)
        ms = jnp.sum(functools.reduce(lambda u, v: u + v, sq), axis=-1, keepdims=True) * (1.0 / D_INNER)
        o_ref[rows, :] = (y_ref[rows, :] * lax.rsqrt(ms + EPS) * nw_ref[...]).astype(o_ref.dtype)


def _conv_shift_matrix():
    pad = (D_CONV - 1) // 2
    taps = [k for k in range(D_CONV) if k != pad]
    m = np.zeros((len(taps) * CHUNK, SHIFT_K), np.float32)
    for i, k in enumerate(taps):
        for l in range(CHUNK):
            m[i * CHUNK + l, CONV_HALO + l + k - pad] = 1.0
    return jnp.asarray(m, BF16)


def _ssd(xbc, dt, zs, conv_w, conv_b, a_row, dskip, ssd_norm_w, tri_lo, tri_up, e2_f, e2_b):
    b, s, _ = xbc.shape
    shift = _conv_shift_matrix()
    rows = CHUNK * CHUNKS_PER_STEP
    assert s % rows == 0
    nb = s // rows
    hb = rows // CONV_HALO
    n_halo = s // CONV_HALO
    row = lambda bi, c: (bi, c, 0)
    params = pltpu.CompilerParams(dimension_semantics=("parallel", "arbitrary"), vmem_limit_bytes=VMEM_LIMIT)
    xc, yf = pl.pallas_call(
        _ssd_fwd_kernel,
        grid=(b, nb),
        in_specs=[
            pl.BlockSpec((None, rows, CONV_DIM), row),
            pl.BlockSpec((None, CONV_HALO, CONV_DIM), lambda bi, c: (bi, jnp.maximum(c * hb - 1, 0), 0)),
            pl.BlockSpec((None, CONV_HALO, CONV_DIM), lambda bi, c: (bi, jnp.minimum((c + 1) * hb, n_halo - 1), 0)),
            pl.BlockSpec((None, rows, LANES), row),
            _const_spec(shift.shape), _const_spec(conv_w.shape), _const_spec(conv_b.shape),
            _const_spec(a_row.shape), _const_spec(dskip.shape), _const_spec(tri_lo.shape), _const_spec(e2_f.shape),
        ],
        out_specs=[pl.BlockSpec((None, rows, CONV_DIM), row), pl.BlockSpec((None, rows, D_INNER), row)],
        out_shape=[jax.ShapeDtypeStruct((b, s, CONV_DIM), BF16), jax.ShapeDtypeStruct((b, s, D_INNER), F32)],
        scratch_shapes=[pltpu.VMEM((SSD_N, D_INNER), F32)],
        compiler_params=params,
        name="ssd_fwd",
    )(xbc, xbc, xbc, dt, shift, conv_w, conv_b, a_row, dskip, tri_lo, e2_f)

    rev = lambda bi, c: (bi, nb - 1 - c, 0)
    return pl.pallas_call(
        _ssd_bwd_kernel,
        grid=(b, nb),
        in_specs=[
            pl.BlockSpec((None, rows, CONV_DIM), rev),
            pl.BlockSpec((None, rows, LANES), rev),
            pl.BlockSpec((None, rows, D_INNER), rev),
            pl.BlockSpec((None, rows, D_INNER), rev),
            _const_spec(a_row.shape), _const_spec(ssd_norm_w.shape), _const_spec(tri_up.shape),
            _const_spec(e2_b.shape),
        ],
        out_specs=pl.BlockSpec((None, rows, D_INNER), rev),
        out_shape=jax.ShapeDtypeStruct((b, s, D_INNER), BF16),
        scratch_shapes=[pltpu.VMEM((SSD_N, D_INNER), F32), pltpu.VMEM((rows, D_INNER), F32)],
        compiler_params=params,
        name="ssd_bwd",
    )(xc, dt, yf, zs, a_row, ssd_norm_w, tri_up, e2_b)


def _resident_spec(shape):
    return pl.BlockSpec(shape, lambda *_: (0,) * len(shape), pipeline_mode=pl.Buffered(1))


def _merge_mlp_kernel(attn_ref, ssd_ref, g_ref, x_ref, gate1_ref, shift2_ref, scale2_ref, gate2_ref, nw2_ref,
                      wao_ref, wso_ref, wo_ref, w1_ref, w2_ref, o_ref):
    a = _dot(attn_ref[...], wao_ref[...])
    sm = _dot(ssd_ref[...], wso_ref[...])
    g = g_ref[...].astype(F32)
    merged = (g[:, :D_MODEL] * a + g[:, D_MODEL:] * sm).astype(BF16)
    x1 = x_ref[...] + gate1_ref[...] * _dot(merged, wo_ref[...])
    ms = jnp.mean(x1 * x1, axis=-1, keepdims=True)
    xn = x1 * lax.rsqrt(ms + EPS)
    h2 = (xn * (nw2_ref[...] * (1.0 + scale2_ref[...])) + shift2_ref[...]).astype(BF16)
    acc = jnp.zeros(x1.shape, F32)
    for blk in range(D_FF // 1024):
        sl = slice(blk * 1024, (blk + 1) * 1024)
        u = jnp.maximum(_dot(h2, w1_ref[:, sl]), 0.0)
        acc = acc + _dot((u * u).astype(BF16), w2_ref[sl, :])
    o_ref[...] = x1 + gate2_ref[...] * acc


def _merge_mlp(attn, ssd, gates, x, gate1, shift2, scale2, gate2, norm2_w, wao, wso, wo, w1, w2, tm):
    b, s, d = x.shape
    row = lambda bi, i: (bi, i, 0)
    per_b = pl.BlockSpec((None, 1, d), lambda bi, i: (bi, 0, 0))
    return pl.pallas_call(
        _merge_mlp_kernel,
        grid=(b, s // tm),
        in_specs=[
            pl.BlockSpec((None, tm, attn.shape[-1]), row), pl.BlockSpec((None, tm, D_INNER), row),
            pl.BlockSpec((None, tm, 2 * d), row), pl.BlockSpec((None, tm, d), row),
            per_b, per_b, per_b, per_b, _const_spec((1, d)),
            _resident_spec(wao.shape), _resident_spec(wso.shape), _resident_spec(wo.shape),
            _resident_spec(w1.shape), _resident_spec(w2.shape),
        ],
        out_specs=pl.BlockSpec((None, tm, d), row),
        out_shape=jax.ShapeDtypeStruct((b, s, d), F32),
        compiler_params=pltpu.CompilerParams(dimension_semantics=("parallel", "parallel"),
                                             vmem_limit_bytes=VMEM_LIMIT),
        name="merge_mlp",
    )(attn, ssd, gates, x, gate1, shift2, scale2, gate2, norm2_w, wao, wso, wo, w1, w2)


def _swap_halves(w):
    return w.reshape(2, 2, 16)[:, ::-1, :].reshape(HEAD_DIM)


def _rope_tables(s, norm_w, scale):
    pos = jnp.arange(s, dtype=jnp.int32)
    pos_row = (pos // GRID_W).astype(F32)
    pos_col = (pos % GRID_W).astype(F32)
    axis_dim = HEAD_DIM // 2
    inv_freq = ROPE_THETA ** (-jnp.arange(0, axis_dim, 2, dtype=F32) / axis_dim)
    ang_r = pos_row[:, None] * inv_freq[None, :]
    ang_c = pos_col[:, None] * inv_freq[None, :]
    cos = jnp.concatenate([jnp.cos(ang_r)] * 2 + [jnp.cos(ang_c)] * 2, axis=-1)
    sin_r, sin_c = jnp.sin(ang_r), jnp.sin(ang_c)
    sin = jnp.concatenate([-sin_r, sin_r, -sin_c, sin_c], axis=-1)
    ta = cos * norm_w[None, :] * scale
    tb = sin * _swap_halves(norm_w)[None, :] * scale
    return jnp.tile(ta, (1, 2)), jnp.tile(tb, (1, 2))


def _head_expansion(slot0):
    e = np.zeros((LANES, D_INNER), np.float32)
    for h in range(SSD_H):
        e[slot0 + h, h * SSD_P:(h + 1) * SSD_P] = 1.0
    return jnp.asarray(np.concatenate([e, e], axis=0), BF16)


def _pick_tile(s, pref):
    t = min(pref, s)
    assert s % t == 0
    return t


def kernel(x, c, w_ada, b_ada, norm1_w, norm2_w, w_in, q_norm_w, k_norm_w, conv_w, conv_b, A_log, dt_bias,
           ssd_D, ssd_norm_w, w_attn_out, w_ssd_out, w_o, w_mlp1, w_mlp2):
    b, s, d = x.shape
    depth = w_ada.shape[0]
    assert d == D_MODEL and s % (CHUNK * CHUNKS_PER_STEP) == 0 and s % GRID_W == 0

    qdim = N_Q_HEADS * HEAD_DIM
    kvdim = N_KV_HEADS * HEAD_DIM
    o_k, o_v, o_x = qdim, qdim + kvdim, qdim + 2 * kvdim
    o_z = o_x + CONV_DIM
    o_dt = o_z + D_INNER
    o_g = o_dt + 2 * SSD_H

    gsum = jnp.asarray(np.kron(np.eye(4, dtype=np.float32), np.full((HEAD_DIM, HEAD_DIM), 1.0 / HEAD_DIM,
                                                                       np.float32)), BF16)
    tri = np.tril(np.ones((CHUNK, CHUNK), np.float32))
    tri_lo = jnp.asarray(tri, BF16)
    tri_up = jnp.asarray(tri.T, BF16)
    e2_f = _head_expansion(0)
    e2_b = _head_expansion(SSD_H)
    ts = _pick_tile(s, 512)
    tq = _pick_tile(s, 256)
    tk = _pick_tile(s, 256)

    for l in range(depth):
        mod = _ada(c, w_ada[l], b_ada[l])
        shift1, scale1, gate1, shift2, scale2, gate2 = [m[:, None, :] for m in jnp.split(mod, 6, axis=-1)]

        w = w_in[l]
        wq = w[:, :o_k].astype(BF16)
        wk = w[:, o_k:o_v].astype(BF16)
        wv = w[:, o_v:o_x].T.astype(BF16)
        wx = w[:, o_x:o_z].astype(BF16)
        wz = w[:, o_z:o_dt].astype(BF16)
        wdt = jnp.pad(w[:, o_dt:o_g], ((0, 0), (0, LANES - 2 * SSD_H))).astype(BF16)
        wg = w[:, o_g:].astype(BF16)
        dtb = jnp.pad(dt_bias[l].reshape(1, 2 * SSD_H), ((0, 0), (0, LANES - 2 * SSD_H)))
        q_scale = math.log2(math.e) * HEAD_DIM ** -0.5
        qa, qb = _rope_tables(s, q_norm_w[l], q_scale)
        ka, kb = _rope_tables(s, k_norm_w[l], 1.0)

        qt, k, vt, xbc, zs, dt, gates = _inproj(x, shift1, scale1, norm1_w[l].reshape(1, d), wq, wk, wv, wx, wz,
                                                wdt, wg, gsum, qa, qb, ka, kb, dtb, ts)
        score_bound = (q_scale * HEAD_DIM * jnp.max(jnp.abs(q_norm_w[l])) * jnp.max(jnp.abs(k_norm_w[l])))
        attn = lax.cond(score_bound <= MAX_UNSHIFTED_SCORE,
                        lambda *a: _attention(*a, tq, tk, _attn_unshifted_kernel),
                        lambda *a: _attention(*a, tq, tk, _attn_kernel), qt, k, vt)

        a_row = jnp.pad(-jnp.exp(A_log[l].astype(F32)).reshape(1, 2 * SSD_H), ((0, 0), (0, LANES - 2 * SSD_H)))
        dskip = jnp.repeat(ssd_D[l].astype(F32), SSD_P).reshape(1, D_INNER)
        ssd = _ssd(xbc, dt, zs, conv_w[l], conv_b[l].reshape(1, CONV_DIM), a_row, dskip,
                   ssd_norm_w[l].reshape(1, D_INNER), tri_lo, tri_up, e2_f, e2_b)

        x = _merge_mlp(attn, ssd, gates, x, gate1, shift2, scale2, gate2, norm2_w[l].reshape(1, d),
                       w_attn_out[l].astype(BF16), w_ssd_out[l].astype(BF16), w_o[l].astype(BF16),
                       w_mlp1[l].astype(BF16), w_mlp2[l].astype(BF16), ts)
    return x
```
